```python
import jax, jax.numpy as jnp
from jax import lax
import numpy as np

D_MODEL = 1024
BATCH = 4
SEQ = 8192
DEPTH = 2

CHUNK = 64
MEM_TOKENS = 256
RMS_EPS = 1e-6
XA_HEADS = 4
XA_HEAD_DIM = D_MODEL // XA_HEADS
HGRN_HEADS = 4
HGRN_DK = 128
HGRN_DV = 128
HGRN_WIDTH = HGRN_HEADS * HGRN_DV
GLA_HEADS = 4
GLA_DK = 64
GLA_DV = 128
GLA_RANK = 16
GLA_GATE_NORMALIZER = 16.0
GLA_WIDTH = GLA_HEADS * GLA_DV
MLSTM_HEADS = 4
MLSTM_DH = 128
MLSTM_WIDTH = MLSTM_HEADS * MLSTM_DH
MLSTM_CONV = 4
MLSTM_QKV_BLOCK = 4

MIX_WIDTH = HGRN_WIDTH + GLA_WIDTH + MLSTM_WIDTH
IN_SPLITS = [
    HGRN_HEADS * HGRN_DK,
    HGRN_HEADS * HGRN_DK,
    HGRN_WIDTH,
    HGRN_WIDTH,
    GLA_HEADS * GLA_DK,
    GLA_HEADS * GLA_DK,
    GLA_WIDTH,
    GLA_RANK,
    GLA_WIDTH,
    MLSTM_WIDTH,
    MLSTM_WIDTH,
    MLSTM_HEADS,
    MLSTM_HEADS,
]
IN_COLS = 4 * HGRN_WIDTH + 2 * GLA_HEADS * GLA_DK + 2 * GLA_WIDTH + GLA_RANK + 2 * MLSTM_WIDTH + 2 * MLSTM_HEADS

kernel_name = "hybrid_hgrn2_gla_mlstm_memxattn"


def rms_norm(x, g):
    xf = x.astype(jnp.float32)
    y = xf * lax.rsqrt(jnp.mean(xf * xf, axis=-1, keepdims=True) + RMS_EPS)
    return (y * g.astype(jnp.float32)).astype(x.dtype)


def head_rms_norm(o, g):
    y = o * lax.rsqrt(jnp.mean(o * o, axis=-1, keepdims=True) + RMS_EPS) * g.astype(jnp.float32)
    return y.reshape(o.shape[0], o.shape[1], -1)


def head_layer_norm(o, g):
    mu = jnp.mean(o, axis=-1, keepdims=True)
    oc = o - mu
    y = oc * lax.rsqrt(jnp.mean(oc * oc, axis=-1, keepdims=True) + RMS_EPS)
    return y.reshape(o.shape[0], o.shape[1], -1) * g.astype(jnp.float32)


def split_heads(t, n_heads):
    return t.reshape(t.shape[0], t.shape[1], n_heads, -1)


def to_chunks(t):
    B, S, H, d = t.shape
    return t.reshape(B, S // CHUNK, CHUNK, H, d).transpose(1, 0, 3, 2, 4)


def from_chunks(t):
    n, B, H, C, d = t.shape
    return t.transpose(1, 0, 3, 2, 4).reshape(B, n * C, H, d)


def chunked_decay_linear_attention(q, k, v, log_a):
    B, S, H, dk = q.shape
    dv = v.shape[-1]
    causal = jnp.tril(jnp.ones((CHUNK, CHUNK), dtype=bool))

    def step(state, inp):
        qi, ki, vi, gi = inp
        b = jnp.cumsum(gi, axis=2)
        o_inter = jnp.einsum('bhcd,bhde->bhce', qi * jnp.exp(b), state)
        diff = b[:, :, :, None, :] - b[:, :, None, :, :]
        decay = jnp.where(causal[:, :, None], jnp.exp(jnp.minimum(diff, 0.0)), 0.0)
        scores = jnp.einsum('bhid,bhjd,bhijd->bhij', qi, ki, decay)
        o_intra = jnp.einsum('bhij,bhje->bhie', scores, vi)
        b_last = b[:, :, -1:, :]
        k_dec = ki * jnp.exp(b_last - b)
        new_state = jnp.exp(b_last[:, :, 0, :])[..., None] * state + jnp.einsum('bhjd,bhje->bhde', k_dec, vi)
        return new_state, o_inter + o_intra

    state0 = jnp.zeros((B, H, dk, dv), jnp.float32)
    _, o = lax.scan(step, state0, (to_chunks(q), to_chunks(k), to_chunks(v), to_chunks(log_a)))
    return from_chunks(o)


def chunked_mlstm(q, k, v, i_pre, log_f):
    B, S, H, dk = q.shape
    dv = v.shape[-1]
    causal = jnp.tril(jnp.ones((CHUNK, CHUNK), dtype=bool))

    def gate_chunks(g):
        return g.reshape(B, S // CHUNK, CHUNK, H).transpose(1, 0, 3, 2)

    def step(carry, inp):
        c_state, n_state, m_state = carry
        qi, ki, vi, ii, fi = inp
        b = jnp.cumsum(fi, axis=-1)
        log_d = b[..., :, None] - b[..., None, :] + ii[..., None, :]
        log_d = jnp.where(causal, log_d, -jnp.inf)
        m_inter = b + m_state[..., None]
        m_i = jnp.maximum(m_inter, jnp.max(log_d, axis=-1))
        w_inter = jnp.exp(m_inter - m_i)
        s = jnp.einsum('bhid,bhjd->bhij', qi, ki) * jnp.exp(log_d - m_i[..., None])
        num = w_inter[..., None] * jnp.einsum('bhid,bhde->bhie', qi, c_state) + jnp.einsum('bhij,bhje->bhie', s, vi)
        den = w_inter * jnp.einsum('bhid,bhd->bhi', qi, n_state) + jnp.sum(s, axis=-1)
        h = num / jnp.maximum(jnp.abs(den), jnp.exp(-m_i))[..., None]
        b_last = b[..., -1]
        log_w = b_last[..., None] - b + ii
        m_new = jnp.maximum(b_last + m_state, jnp.max(log_w, axis=-1))
        w_prev = jnp.exp(b_last + m_state - m_new)
        wj = jnp.exp(log_w - m_new[..., None])
        c_new = w_prev[..., None, None] * c_state + jnp.einsum('bhj,bhjd,bhje->bhde', wj, ki, vi)
        n_new = w_prev[..., None] * n_state + jnp.einsum('bhj,bhjd->bhd', wj, ki)
        return (c_new, n_new, m_new), h

    carry0 = (jnp.zeros((B, H, dk, dv), jnp.float32), jnp.zeros((B, H, dk), jnp.float32), jnp.zeros((B, H), jnp.float32))
    _, h = lax.scan(step, carry0, (to_chunks(q), to_chunks(k), to_chunks(v), gate_chunks(i_pre), gate_chunks(log_f)))
    return from_chunks(h)


def causal_depthwise_conv(u, w, b):
    S = u.shape[1]
    up = jnp.pad(u, ((0, 0), (MLSTM_CONV - 1, 0), (0, 0)))
    out = b.astype(jnp.float32)
    for tap in range(MLSTM_CONV):
        out = out + up[:, tap:tap + S, :] * w[tap].astype(jnp.float32)
    return out


def block_diag_proj(t, w):
    B, S, C = t.shape
    tg = t.reshape(B, S, w.shape[0], w.shape[1])
    return jnp.einsum('bsgi,gio->bsgo', tg, w.astype(jnp.float32)).reshape(B, S, C)


def memory_cross_attention(h, m, wq, wk, wv, wo):
    B, S, D = h.shape
    M = m.shape[1]
    q = (h @ wq).reshape(B, S, XA_HEADS, XA_HEAD_DIM)
    k = (m @ wk).reshape(B, M, XA_HEADS, XA_HEAD_DIM)
    v = (m @ wv).reshape(B, M, XA_HEADS, XA_HEAD_DIM)
    s = jnp.einsum('bshd,bmhd->bhsm', q, k).astype(jnp.float32) * (XA_HEAD_DIM ** -0.5)
    p = jax.nn.softmax(s, axis=-1).astype(v.dtype)
    o = jnp.einsum('bhsm,bmhd->bshd', p, v).reshape(B, S, D)
    return o @ wo


def setup_inputs(seed: int = 0) -> dict:
    key = jax.random.key(seed)
    ks = jax.random.split(key, 26)

    def nrm(k, shape, scale):
        return jax.random.normal(k, shape, jnp.float32) * scale

    def gain(k, shape):
        return 1.0 + nrm(k, shape, 0.02)

    n_blk = MLSTM_WIDTH // MLSTM_QKV_BLOCK
    return {
        "x": nrm(ks[0], (BATCH, SEQ, D_MODEL), 1.0),
        "mem": nrm(ks[1], (BATCH, MEM_TOKENS, D_MODEL), 1.0),
        "norm_mix": gain(ks[2], (DEPTH, D_MODEL)),
        "w_in": nrm(ks[3], (DEPTH, D_MODEL, IN_COLS), D_MODEL ** -0.5),
        "hgrn_lb_logits": nrm(ks[4], (DEPTH, HGRN_HEADS * HGRN_DK), 0.5),
        "hgrn_norm": gain(ks[5], (DEPTH, HGRN_DV)),
        "gla_gate_up": nrm(ks[6], (DEPTH, GLA_RANK, GLA_HEADS * GLA_DK), GLA_RANK ** -0.5),
        "gla_gate_bias": nrm(ks[7], (DEPTH, GLA_HEADS * GLA_DK), 0.1),
        "gla_norm": gain(ks[8], (DEPTH, GLA_DV)),
        "mlstm_conv_w": nrm(ks[9], (DEPTH, MLSTM_CONV, MLSTM_WIDTH), MLSTM_CONV ** -0.5),
        "mlstm_conv_b": nrm(ks[10], (DEPTH, MLSTM_WIDTH), 0.02),
        "mlstm_wq": nrm(ks[11], (DEPTH, n_blk, MLSTM_QKV_BLOCK, MLSTM_QKV_BLOCK), MLSTM_QKV_BLOCK ** -0.5),
        "mlstm_wk": nrm(ks[12], (DEPTH, n_blk, MLSTM_QKV_BLOCK, MLSTM_QKV_BLOCK), MLSTM_QKV_BLOCK ** -0.5),
        "mlstm_wv": nrm(ks[13], (DEPTH, n_blk, MLSTM_QKV_BLOCK, MLSTM_QKV_BLOCK), MLSTM_QKV_BLOCK ** -0.5),
        "mlstm_igate_bias": nrm(ks[14], (DEPTH, MLSTM_HEADS), 0.1),
        "mlstm_fgate_bias": jnp.linspace(3.0, 6.0, MLSTM_HEADS, dtype=jnp.float32)[None, :] + nrm(ks[15], (DEPTH, MLSTM_HEADS), 0.1),
        "mlstm_skip": gain(ks[16], (DEPTH, MLSTM_WIDTH)),
        "mlstm_norm": gain(ks[17], (DEPTH, MLSTM_WIDTH)),
        "w_out": nrm(ks[18], (DEPTH, MIX_WIDTH, D_MODEL), MIX_WIDTH ** -0.5),
        "norm_xattn": gain(ks[19], (DEPTH, D_MODEL)),
        "norm_mem": gain(ks[20], (DEPTH, D_MODEL)),
        "xa_wq": nrm(ks[21], (DEPTH, D_MODEL, D_MODEL), D_MODEL ** -0.5),
        "xa_wk": nrm(ks[22], (DEPTH, D_MODEL, D_MODEL), D_MODEL ** -0.5),
        "xa_wv": nrm(ks[23], (DEPTH, D_MODEL, D_MODEL), D_MODEL ** -0.5),
        "xa_wo": nrm(ks[24], (DEPTH, D_MODEL, D_MODEL), D_MODEL ** -0.5),
        "norm_final": gain(ks[25], (D_MODEL,)),
    }


def reference(x, mem, norm_mix, w_in, hgrn_lb_logits, hgrn_norm, gla_gate_up, gla_gate_bias, gla_norm,
              mlstm_conv_w, mlstm_conv_b, mlstm_wq, mlstm_wk, mlstm_wv, mlstm_igate_bias, mlstm_fgate_bias,
              mlstm_skip, mlstm_norm, w_out, norm_xattn, norm_mem, xa_wq, xa_wk, xa_wv, xa_wo, norm_final):
    B, S, _ = x.shape
    split_idx = np.cumsum(IN_SPLITS)[:-1].tolist()
    lb_all = jnp.cumsum(jax.nn.softmax(hgrn_lb_logits.astype(jnp.float32), axis=0), axis=0)
    lb_all = lb_all - lb_all[0:1]

    for l in range(DEPTH):
        h = rms_norm(x, norm_mix[l])
        proj = jnp.einsum('bsd,dc->bsc', h, w_in[l]).astype(jnp.float32)
        (a_q, a_f, a_i, a_z, g_q, g_k, g_v, g_a, g_z, m_u, m_z, m_i, m_f) = jnp.split(proj, split_idx, axis=-1)

        lb = lb_all[l]
        log_f_a = jnp.logaddexp(jnp.log(lb), jnp.log1p(-lb) + jax.nn.log_sigmoid(a_f))
        k_a = (1.0 - lb) * jax.nn.sigmoid(-a_f)
        o_a = chunked_decay_linear_attention(split_heads(jax.nn.silu(a_q), HGRN_HEADS), split_heads(k_a, HGRN_HEADS),
                                             split_heads(a_i, HGRN_HEADS), split_heads(log_f_a, HGRN_HEADS))
        o_a = head_rms_norm(o_a, hgrn_norm[l]) * jax.nn.silu(a_z)

        log_alpha = jax.nn.log_sigmoid(g_a @ gla_gate_up[l].astype(jnp.float32) + gla_gate_bias[l].astype(jnp.float32)) / GLA_GATE_NORMALIZER
        o_b = chunked_decay_linear_attention(split_heads(g_q * (GLA_DK ** -0.5), GLA_HEADS), split_heads(g_k, GLA_HEADS),
                                             split_heads(g_v, GLA_HEADS), split_heads(log_alpha, GLA_HEADS))
        o_b = head_rms_norm(o_b, gla_norm[l]) * jax.nn.silu(g_z)

        conv = jax.nn.silu(causal_depthwise_conv(m_u, mlstm_conv_w[l], mlstm_conv_b[l]))
        q_c = block_diag_proj(conv, mlstm_wq[l])
        k_c = block_diag_proj(conv, mlstm_wk[l]) * (MLSTM_DH ** -0.5)
        v_c = block_diag_proj(m_u, mlstm_wv[l])
        i_pre = m_i + mlstm_igate_bias[l].astype(jnp.float32)
        log_f_c = jax.nn.log_sigmoid(m_f + mlstm_fgate_bias[l].astype(jnp.float32))
        o_c = chunked_mlstm(split_heads(q_c, MLSTM_HEADS), split_heads(k_c, MLSTM_HEADS), split_heads(v_c, MLSTM_HEADS), i_pre, log_f_c)
        o_c = (head_layer_norm(o_c, mlstm_norm[l]) + mlstm_skip[l].astype(jnp.float32) * conv) * jax.nn.silu(m_z)

        mix = jnp.concatenate([o_a, o_b, o_c], axis=-1).astype(x.dtype)
        x = x + jnp.einsum('bsc,cd->bsd', mix, w_out[l])

        x = x + memory_cross_attention(rms_norm(x, norm_xattn[l]), rms_norm(mem, norm_mem[l]),
                                       xa_wq[l], xa_wk[l], xa_wv[l], xa_wo[l])

    return rms_norm(x, norm_final)
```

```python
import functools

import numpy as np
import jax
import jax.numpy as jnp
from jax import lax
from jax.experimental import pallas as pl
from jax.experimental.pallas import tpu as pltpu

F32 = jnp.float32
BF16 = jnp.bfloat16

RMS_EPS = 1e-6
D_MODEL = 1024
CHUNK = 64
LANES = 128
N_HEADS = 4
HGRN_D = 128
GLA_DK = 64
GLA_DV = 128
GLA_RANK = 16
GLA_GATE_NORMALIZER = 16.0
MLSTM_DH = 128
MLSTM_CONV = 4
XA_HEAD_DIM = D_MODEL // N_HEADS
MIX_WIDTH = 3 * N_HEADS * 128

C_AQ, C_AF, C_AI, C_AZ = 0, 512, 1024, 1536
C_GQ, C_GK, C_GV, C_GZ = 2048, 2304, 2560, 3072
C_MU, C_MZ = 3584, 4096
C_SMALL = 4608
PROJ_COLS = C_SMALL + LANES
L_GA = 0
L_MI = GLA_RANK
L_MF = GLA_RANK + N_HEADS

N_LEVELS = 6
VMEM_LIMIT = 48 * 1024 * 1024


def _decay_sum_matrix():
    t = np.arange(CHUNK)[:, None]
    s = np.arange(CHUNK)[None, :]
    blocks = [(s <= t).astype(np.float32), (s > t).astype(np.float32)]
    for lvl in range(N_LEVELS):
        m = CHUNK >> (lvl + 1)
        ref = (t // (2 * m)) * (2 * m) + m - 1
        upper = (t % (2 * m)) >= m
        up_rows = ((s > ref) & (s <= t)).astype(np.float32)
        lo_rows = ((s > t) & (s <= ref)).astype(np.float32)
        blocks.append(np.where(upper, up_rows, lo_rows))
    return np.concatenate(blocks, axis=0)


def _sigmoid(x):
    return 1.0 / (1.0 + jnp.exp(-x))


def _silu(x):
    return x * _sigmoid(x)


def _log_sigmoid(x):
    return jnp.minimum(x, 0.0) - jnp.log1p(jnp.exp(-jnp.abs(x)))


def _split_bf16(x):
    hi = x.astype(BF16)
    lo = (x - hi.astype(F32)).astype(BF16)
    return hi, lo


def _dot(a, b):
    return jnp.dot(a, b, preferred_element_type=F32)


def _dot_nt(a, b):
    return lax.dot_general(a, b, (((1,), (1,)), ((), ())), preferred_element_type=F32)


def _dot_tn(a, b):
    return lax.dot_general(a, b, (((0,), (0,)), ((), ())), preferred_element_type=F32)


def _exact_sum_dot(m_bf16, x):
    hi = x.astype(BF16)
    r1 = x - hi.astype(F32)
    mid = r1.astype(BF16)
    lo = (r1 - mid.astype(F32)).astype(BF16)
    return _dot(m_bf16, hi) + _dot(m_bf16, mid) + _dot(m_bf16, lo)


def _inproj_kernel(x_ref, g_ref, w_ref, o_ref):
    x = x_ref[...]
    h = x * lax.rsqrt(jnp.mean(x * x, axis=-1, keepdims=True) + RMS_EPS) * g_ref[...]
    o_ref[...] = _dot(h.astype(BF16), w_ref[...])


def _inproj(x2d, g, w, tm):
    n = x2d.shape[0]
    return pl.pallas_call(
        _inproj_kernel,
        out_shape=jax.ShapeDtypeStruct((n, PROJ_COLS), F32),
        grid=(n // tm,),
        in_specs=[
            pl.BlockSpec((tm, D_MODEL), lambda i: (i, 0)),
            pl.BlockSpec((1, D_MODEL), lambda i: (0, 0)),
            pl.BlockSpec((D_MODEL, PROJ_COLS), lambda i: (0, 0)),
        ],
        out_specs=pl.BlockSpec((tm, PROJ_COLS), lambda i: (i, 0)),
        compiler_params=pltpu.CompilerParams(
            dimension_semantics=("arbitrary",), vmem_limit_bytes=VMEM_LIMIT),
        name="inproj",
    )(x2d, g, w)


def _memkv_kernel(m_ref, g_ref, wk_ref, wv_ref, k_ref, v_ref):
    x = m_ref[...]
    h = (x * lax.rsqrt(jnp.mean(x * x, axis=-1, keepdims=True) + RMS_EPS) * g_ref[...]).astype(BF16)
    k_ref[...] = _dot(h, wk_ref[...]).astype(BF16)
    v_ref[...] = _dot(h, wv_ref[...]).astype(BF16)


def _memkv(mem2d, g, wk, wv, tm):
    n = mem2d.shape[0]
    full = pl.BlockSpec((D_MODEL, D_MODEL), lambda i: (0, 0))
    tile = pl.BlockSpec((tm, D_MODEL), lambda i: (i, 0))
    return pl.pallas_call(
        _memkv_kernel,
        out_shape=(jax.ShapeDtypeStruct((n, D_MODEL), BF16),) * 2,
        grid=(n // tm,),
        in_specs=[tile, pl.BlockSpec((1, D_MODEL), lambda i: (0, 0)), full, full],
        out_specs=(tile, tile),
        compiler_params=pltpu.CompilerParams(
            dimension_semantics=("arbitrary",), vmem_limit_bytes=VMEM_LIMIT),
        name="memkv",
    )(mem2d, g, wk, wv)


def _level_masks():
    i = lax.broadcasted_iota(jnp.int32, (CHUNK, CHUNK), 0)
    j = lax.broadcasted_iota(jnp.int32, (CHUNK, CHUNK), 1)
    x = jnp.bitwise_xor(i, j)
    below = i > j
    masks = []
    for lvl in range(N_LEVELS):
        top_bit = N_LEVELS - 1 - lvl
        masks.append(jnp.logical_and(below, jnp.right_shift(x, top_bit) == 1))
    return i == j, masks


def _decay_attention_head(q, k, v, e_all, st, lane_mask, diag, masks):
    def pick(a):
        return a if lane_mask is None else jnp.where(lane_mask, a, 0.0)

    vb = v.astype(BF16)
    q_inter = (pick(q) * jnp.exp(e_all[0:CHUNK])).astype(BF16)
    o = _dot_nt(q_inter, st.astype(BF16))
    scores = jnp.where(diag, _dot_nt(pick(q).astype(BF16), k.astype(BF16)), 0.0)
    for lvl in range(N_LEVELS):
        ee = jnp.exp(e_all[(2 + lvl) * CHUNK:(3 + lvl) * CHUNK])
        p = _dot_nt((pick(q) * ee).astype(BF16), (k * ee).astype(BF16))
        scores = jnp.where(masks[lvl], p, scores)
    o = o + _dot(scores.astype(BF16), vb)
    k_state = (k * jnp.exp(e_all[CHUNK:2 * CHUNK])).astype(BF16)
    return o, _dot_tn(vb, k_state)


def _mixer_kernel(proj_ref, msum_ref, lb_logit_ref, hnorm_ref, gup_ref, gbias_ref, gnorm_ref,
                  convw_ref, convb_ref, wq_ref, wk_ref, wv_ref, sbias_ref, skip_ref, mnorm_ref,
                  out_ref,
                  hst_ref, gst_ref, mc_ref, mn_ref, mm_ref, conv_ref, *, layer):
    @pl.when(pl.program_id(1) == 0)
    def _():
        hst_ref[...] = jnp.zeros_like(hst_ref)
        gst_ref[...] = jnp.zeros_like(gst_ref)
        mc_ref[...] = jnp.zeros_like(mc_ref)
        mn_ref[...] = jnp.zeros_like(mn_ref)
        mm_ref[...] = jnp.zeros_like(mm_ref)
        conv_ref[0:8, :] = jnp.zeros((8, N_HEADS * MLSTM_DH), F32)

    msum = msum_ref[...]
    diag, masks = _level_masks()
    lane = lax.broadcasted_iota(jnp.int32, (CHUNK, LANES), 1)

    logits = lb_logit_ref[...]
    ex = jnp.exp(logits - jnp.max(logits, axis=0, keepdims=True))
    soft = ex / jnp.sum(ex, axis=0, keepdims=True)
    c0 = soft[0:1, :]
    cl = c0
    for r in range(1, layer + 1):
        cl = cl + soft[r:r + 1, :]
    lb = cl - c0
    log_lb = jnp.log(lb)
    log_1m_lb = jnp.log1p(-lb)

    for h in range(N_HEADS):
        sl = slice(h * HGRN_D, (h + 1) * HGRN_D)
        a_f = proj_ref[:, C_AF + h * HGRN_D:C_AF + (h + 1) * HGRN_D]
        lsig = _log_sigmoid(a_f)
        t2 = log_1m_lb[:, sl] + lsig
        t1 = log_lb[:, sl]
        log_f = jnp.maximum(t1, t2) + jnp.log1p(jnp.exp(-jnp.abs(t1 - t2)))
        k = (1.0 - lb[:, sl]) * _sigmoid(-a_f)
        q = _silu(proj_ref[:, C_AQ + h * HGRN_D:C_AQ + (h + 1) * HGRN_D])
        v = proj_ref[:, C_AI + h * HGRN_D:C_AI + (h + 1) * HGRN_D]
        e_all = _exact_sum_dot(msum, log_f)
        st = hst_ref[h]
        o, inc = _decay_attention_head(q, k, v, e_all, st, None, diag, masks)
        hst_ref[h] = jnp.exp(e_all[CHUNK - 1:CHUNK]) * st + inc
        y = o * lax.rsqrt(jnp.mean(o * o, axis=-1, keepdims=True) + RMS_EPS) * hnorm_ref[...]
        y = y * _silu(proj_ref[:, C_AZ + h * HGRN_D:C_AZ + (h + 1) * HGRN_D])
        out_ref[:, h * 128:(h + 1) * 128] = y.astype(out_ref.dtype)

    small = proj_ref[:, C_SMALL:C_SMALL + LANES]
    pre = _dot(small.astype(BF16), gup_ref[...]) + gbias_ref[...]
    log_alpha = _log_sigmoid(pre) * (1.0 / GLA_GATE_NORMALIZER)
    for pair in range(N_HEADS // 2):
        psl = slice(pair * LANES, (pair + 1) * LANES)
        q = proj_ref[:, C_GQ + pair * LANES:C_GQ + (pair + 1) * LANES] * (GLA_DK ** -0.5)
        k = proj_ref[:, C_GK + pair * LANES:C_GK + (pair + 1) * LANES]
        e_all = _exact_sum_dot(msum, log_alpha[:, psl])
        st = gst_ref[pair]
        incs = []
        for hp in range(2):
            h = 2 * pair + hp
            lane_mask = (lane < GLA_DK) if hp == 0 else (lane >= GLA_DK)
            v = proj_ref[:, C_GV + h * GLA_DV:C_GV + (h + 1) * GLA_DV]
            o, inc = _decay_attention_head(q, k, v, e_all, st, lane_mask, diag, masks)
            incs.append(inc)
            y = o * lax.rsqrt(jnp.mean(o * o, axis=-1, keepdims=True) + RMS_EPS) * gnorm_ref[...]
            y = y * _silu(proj_ref[:, C_GZ + h * GLA_DV:C_GZ + (h + 1) * GLA_DV])
            out_ref[:, 512 + h * 128:512 + (h + 1) * 128] = y.astype(out_ref.dtype)
        lane_st = lax.broadcasted_iota(jnp.int32, (GLA_DV, LANES), 1)
        gst_ref[pair] = (jnp.exp(e_all[CHUNK - 1:CHUNK]) * st
                         + jnp.where(lane_st < GLA_DK, incs[0], incs[1]))

    m_u = proj_ref[:, C_MU:C_MU + N_HEADS * MLSTM_DH]
    conv_ref[8:8 + CHUNK, :] = m_u
    acc = convb_ref[...]
    for tap in range(MLSTM_CONV):
        off = 8 - (MLSTM_CONV - 1) + tap
        acc = acc + conv_ref[off:off + CHUNK, :] * convw_ref[tap:tap + 1, :]
    conv_ref[0:8, :] = m_u[CHUNK - 8:CHUNK, :]
    conv = _silu(acc)

    sb = small + sbias_ref[...]
    is_f = jnp.logical_and(lane >= L_MF, lane < L_MF + N_HEADS)
    is_i = jnp.logical_and(lane >= L_MI, lane < L_MI + N_HEADS)
    log_fc = jnp.where(is_f, _log_sigmoid(sb), 0.0)
    b_col = _exact_sum_dot(msum[0:CHUNK], log_fc)
    z = jnp.where(is_i, sb, b_col)
    eye = (lax.broadcasted_iota(jnp.int32, (LANES, LANES), 0)
           == lax.broadcasted_iota(jnp.int32, (LANES, LANES), 1)).astype(BF16)
    zh = z.astype(BF16)
    zr = z - zh.astype(F32)
    zm = zr.astype(BF16)
    zl = (zr - zm.astype(F32)).astype(BF16)
    z_t = _dot_nt(eye, zh) + _dot_nt(eye, zm) + _dot_nt(eye, zl)

    ci = lax.broadcasted_iota(jnp.int32, (CHUNK, CHUNK), 0)
    cj = lax.broadcasted_iota(jnp.int32, (CHUNK, CHUNK), 1)
    causal = ci >= cj
    for h in range(N_HEADS):
        sl = slice(h * MLSTM_DH, (h + 1) * MLSTM_DH)
        cb = conv[:, sl].astype(BF16)
        q = _dot(cb, wq_ref[h])
        k = _dot(cb, wk_ref[h]) * (MLSTM_DH ** -0.5)
        v = _dot(m_u[:, sl].astype(BF16), wv_ref[h])
        qb, kb, vb = q.astype(BF16), k.astype(BF16), v.astype(BF16)
        bc = z[:, L_MF + h:L_MF + h + 1]
        ic = z[:, L_MI + h:L_MI + h + 1]
        br = z_t[L_MF + h:L_MF + h + 1, :]
        ir = z_t[L_MI + h:L_MI + h + 1, :]
        m_prev = mm_ref[h:h + 1, 0:1]
        c_prev = mc_ref[h]
        n_prev = mn_ref[h:h + 1, :]

        log_d = jnp.where(causal, bc - br + ir, -jnp.inf)
        m_inter = bc + m_prev
        m_i = jnp.maximum(m_inter, jnp.max(log_d, axis=-1, keepdims=True))
        w_inter = jnp.exp(m_inter - m_i)
        s = _dot_nt(qb, kb) * jnp.exp(log_d - m_i)
        num = w_inter * _dot(qb, c_prev.astype(BF16)) + _dot(s.astype(BF16), vb)
        den = (w_inter * jnp.sum(q * n_prev, axis=-1, keepdims=True)
               + jnp.sum(s, axis=-1, keepdims=True))
        hh = num / jnp.maximum(jnp.abs(den), jnp.exp(-m_i))

        b_last = bc[CHUNK - 1:CHUNK, :]
        log_w = b_last - bc + ic
        m_new = jnp.maximum(b_last + m_prev, jnp.max(log_w, axis=0, keepdims=True))
        w_prev = jnp.exp(b_last + m_prev - m_new)
        kw = jnp.exp(log_w - m_new) * k
        mc_ref[h] = w_prev * c_prev + _dot_tn(kw.astype(BF16), vb)
        mn_ref[h:h + 1, :] = w_prev * n_prev + jnp.sum(kw, axis=0, keepdims=True)
        mm_ref[h:h + 1, :] = jnp.broadcast_to(m_new, (1, LANES))

        mu = jnp.mean(hh, axis=-1, keepdims=True)
        hc = hh - mu
        y = hc * lax.rsqrt(jnp.mean(hc * hc, axis=-1, keepdims=True) + RMS_EPS) * mnorm_ref[:, sl]
        y = (y + skip_ref[:, sl] * conv[:, sl]) * _silu(
            proj_ref[:, C_MZ + h * MLSTM_DH:C_MZ + (h + 1) * MLSTM_DH])
        out_ref[:, 1024 + h * 128:1024 + (h + 1) * 128] = y.astype(out_ref.dtype)


def _mixers(proj, msum, lb_logits, hnorm, gup, gbias, gnorm, convw, convb, wq, wk, wv,
            sbias, skip, mnorm, layer):
    b, s, _ = proj.shape
    depth = lb_logits.shape[0]

    def whole(shape):
        return pl.BlockSpec(shape, lambda i, j: (0,) * len(shape))

    return pl.pallas_call(
        functools.partial(_mixer_kernel, layer=layer),
        out_shape=jax.ShapeDtypeStruct((b, s, MIX_WIDTH), BF16),
        grid=(b, s // CHUNK),
        in_specs=[
            pl.BlockSpec((None, CHUNK, PROJ_COLS), lambda i, j: (i, j, 0)),
            whole(((2 + N_LEVELS) * CHUNK, CHUNK)),
            whole((depth, N_HEADS * HGRN_D)),
            whole((1, HGRN_D)),
            whole((LANES, N_HEADS * GLA_DK)),
            whole((1, N_HEADS * GLA_DK)),
            whole((1, GLA_DV)),
            whole((MLSTM_CONV, N_HEADS * MLSTM_DH)),
            whole((1, N_HEADS * MLSTM_DH)),
            whole((N_HEADS, MLSTM_DH, MLSTM_DH)),
            whole((N_HEADS, MLSTM_DH, MLSTM_DH)),
            whole((N_HEADS, MLSTM_DH, MLSTM_DH)),
            whole((1, LANES)),
            whole((1, N_HEADS * MLSTM_DH)),
            whole((1, N_HEADS * MLSTM_DH)),
        ],
        out_specs=pl.BlockSpec((None, CHUNK, MIX_WIDTH), lambda i, j: (i, j, 0)),
        scratch_shapes=[
            pltpu.VMEM((N_HEADS, HGRN_D, HGRN_D), F32),
            pltpu.VMEM((N_HEADS // 2, GLA_DV, LANES), F32),
            pltpu.VMEM((N_HEADS, MLSTM_DH, MLSTM_DH), F32),
            pltpu.VMEM((8, MLSTM_DH), F32),
            pltpu.VMEM((8, LANES), F32),
            pltpu.VMEM((8 + CHUNK, N_HEADS * MLSTM_DH), F32),
        ],
        compiler_params=pltpu.CompilerParams(
            dimension_semantics=("arbitrary", "arbitrary"), vmem_limit_bytes=VMEM_LIMIT),
        name="mixers",
    )(proj, msum, lb_logits, hnorm, gup, gbias, gnorm, convw, convb, wq, wk, wv, sbias, skip, mnorm)


def _outattn_kernel(mix_ref, x_ref, wout_ref, gx_ref, wq_ref, k_ref, v_ref, wo_ref, gf_ref,
                    o_ref, *, final_norm):
    x1 = x_ref[...] + _dot(mix_ref[...], wout_ref[...])
    h = (x1 * lax.rsqrt(jnp.mean(x1 * x1, axis=-1, keepdims=True) + RMS_EPS) * gx_ref[...]).astype(BF16)
    q = _dot(h, wq_ref[...])
    outs = []
    for hd in range(N_HEADS):
        sl = slice(hd * XA_HEAD_DIM, (hd + 1) * XA_HEAD_DIM)
        s = _dot_nt(q[:, sl].astype(BF16), k_ref[:, sl]) * (XA_HEAD_DIM ** -0.5)
        e = jnp.exp(s - jnp.max(s, axis=-1, keepdims=True))
        p = e / jnp.sum(e, axis=-1, keepdims=True)
        outs.append(_dot(p.astype(BF16), v_ref[:, sl]).astype(BF16))
    o = jnp.concatenate(outs, axis=-1)
    x2 = x1 + _dot(o, wo_ref[...])
    if final_norm:
        x2 = x2 * lax.rsqrt(jnp.mean(x2 * x2, axis=-1, keepdims=True) + RMS_EPS) * gf_ref[...]
    o_ref[...] = x2


def _outattn(mix, x, wout, gx, wq, kmem, vmem, wo, gf, tm, final_norm):
    b, s, _ = x.shape
    m = kmem.shape[1]

    def whole(shape):
        return pl.BlockSpec(shape, lambda i, j: (0,) * len(shape))

    return pl.pallas_call(
        functools.partial(_outattn_kernel, final_norm=final_norm),
        out_shape=jax.ShapeDtypeStruct((b, s, D_MODEL), F32),
        grid=(b, s // tm),
        in_specs=[
            pl.BlockSpec((None, tm, MIX_WIDTH), lambda i, j: (i, j, 0)),
            pl.BlockSpec((None, tm, D_MODEL), lambda i, j: (i, j, 0)),
            whole((MIX_WIDTH, D_MODEL)),
            whole((1, D_MODEL)),
            whole((D_MODEL, D_MODEL)),
            pl.BlockSpec((None, m, D_MODEL), lambda i, j: (i, 0, 0)),
            pl.BlockSpec((None, m, D_MODEL), lambda i, j: (i, 0, 0)),
            whole((D_MODEL, D_MODEL)),
            whole((1, D_MODEL)),
        ],
        out_specs=pl.BlockSpec((None, tm, D_MODEL), lambda i, j: (i, j, 0)),
        compiler_params=pltpu.CompilerParams(
            dimension_semantics=("arbitrary", "arbitrary"), vmem_limit_bytes=VMEM_LIMIT),
        name="outattn",
    )(mix, x, wout, gx, wq, kmem, vmem, wo, gf)


def _pack_w_in(w):
    ga0 = 2048 + 2 * N_HEADS * GLA_DK + N_HEADS * GLA_DV
    ga1 = ga0 + GLA_RANK
    gate0 = ga1 + N_HEADS * GLA_DV + 2 * N_HEADS * MLSTM_DH
    pad = jnp.zeros((w.shape[0], LANES - GLA_RANK - 2 * N_HEADS), w.dtype)
    return jnp.concatenate([w[:, :ga0], w[:, ga1:gate0], w[:, ga0:ga1], w[:, gate0:], pad],
                           axis=1).astype(BF16)


def _block_diag_dense(w):
    blk = w.shape[1]
    per = MLSTM_DH // blk
    wg = w.reshape(N_HEADS, per, blk, blk)
    eye = jnp.eye(per, dtype=w.dtype)
    dense = wg[:, :, :, None, :] * eye[None, :, None, :, None]
    return dense.reshape(N_HEADS, MLSTM_DH, MLSTM_DH).astype(BF16)


def kernel(x, mem, norm_mix, w_in, hgrn_lb_logits, hgrn_norm, gla_gate_up, gla_gate_bias, gla_norm,
           mlstm_conv_w, mlstm_conv_b, mlstm_wq, mlstm_wk, mlstm_wv, mlstm_igate_bias,
           mlstm_fgate_bias, mlstm_skip, mlstm_norm, w_out, norm_xattn, norm_mem, xa_wq, xa_wk,
           xa_wv, xa_wo, norm_final):
    b, s, d = x.shape
    m = mem.shape[1]
    depth = w_in.shape[0]
    msum = jnp.asarray(_decay_sum_matrix(), dtype=BF16)
    mem2d = mem.reshape(b * m, d)
    zpad = jnp.zeros((LANES - GLA_RANK - 2 * N_HEADS,), F32)

    for l in range(depth):
        proj = _inproj(x.reshape(b * s, d), norm_mix[l][None, :], _pack_w_in(w_in[l]), tm=256)
        gup = jnp.concatenate(
            [gla_gate_up[l], jnp.zeros((LANES - GLA_RANK, N_HEADS * GLA_DK), F32)], axis=0).astype(BF16)
        sbias = jnp.concatenate(
            [jnp.zeros((GLA_RANK,), F32), mlstm_igate_bias[l], mlstm_fgate_bias[l], zpad])[None, :]
        mix = _mixers(
            proj.reshape(b, s, PROJ_COLS), msum, hgrn_lb_logits, hgrn_norm[l][None, :], gup,
            gla_gate_bias[l][None, :], gla_norm[l][None, :], mlstm_conv_w[l],
            mlstm_conv_b[l][None, :], _block_diag_dense(mlstm_wq[l]), _block_diag_dense(mlstm_wk[l]),
            _block_diag_dense(mlstm_wv[l]), sbias, mlstm_skip[l][None, :], mlstm_norm[l][None, :],
            layer=l)
        kmem, vmem = _memkv(mem2d, norm_mem[l][None, :], xa_wk[l].astype(BF16),
                            xa_wv[l].astype(BF16), tm=256)
        x = _outattn(mix, x, w_out[l].astype(BF16), norm_xattn[l][None, :], xa_wq[l].astype(BF16),
                     kmem.reshape(b, m, d), vmem.reshape(b, m, d), xa_wo[l].astype(BF16),
                     norm_final[None, :], tm=min(512, s), final_norm=(l == depth - 1))
    return x
```

```python
import functools

import numpy as np
import jax
import jax.numpy as jnp
from jax import lax
from jax.experimental import pallas as pl
from jax.experimental.pallas import tpu as pltpu

F32 = jnp.float32
BF16 = jnp.bfloat16

RMS_EPS = 1e-6
D_MODEL = 1024
CHUNK = 64
SUB = 8
LANES = 128
N_HEADS = 4
HGRN_D = 128
GLA_DK = 64
GLA_DV = 128
GLA_RANK = 16
GLA_GATE_NORMALIZER = 16.0
MLSTM_DH = 128
MLSTM_CONV = 4
XA_HEAD_DIM = D_MODEL // N_HEADS
MIX_WIDTH = 3 * N_HEADS * 128

C_AQ, C_AF, C_AI, C_AZ = 0, 512, 1024, 1536
C_GQ, C_GK, C_GV, C_GZ = 2048, 2304, 2560, 3072
C_MU, C_MZ = 3584, 4096
C_SMALL = 4608
PROJ_COLS = C_SMALL + LANES
L_MI = GLA_RANK
L_MF = GLA_RANK + N_HEADS

N_LEVELS = 6
LOG_ZERO = -1e30
VMEM_LIMIT = 48 * 1024 * 1024


def _sigmoid(x):
    return 1.0 / (1.0 + jnp.exp(-x))


def _silu(x):
    return x * _sigmoid(x)


def _log_sigmoid(x):
    return jnp.minimum(x, 0.0) - jnp.log(1.0 + jnp.exp(-jnp.abs(x)))


def _dot(a, b):
    return jnp.dot(a, b, preferred_element_type=F32)


def _dot_nt(a, b):
    return lax.dot_general(a, b, (((1,), (1,)), ((), ())), preferred_element_type=F32)


def _dot_tn(a, b):
    return lax.dot_general(a, b, (((0,), (0,)), ((), ())), preferred_element_type=F32)


def _split3(x):
    hi = x.astype(BF16)
    r1 = x - hi.astype(F32)
    mid = r1.astype(BF16)
    lo = (r1 - mid.astype(F32)).astype(BF16)
    return hi, mid, lo


def _cumsum_rows(tri_bf16, x):
    hi, mid, lo = _split3(x)
    return _dot(tri_bf16, hi) + _dot(tri_bf16, mid) + _dot(tri_bf16, lo)


def _inproj_kernel(x_ref, g_ref, w_ref, o_ref):
    x = x_ref[...]
    h = x * lax.rsqrt(jnp.mean(x * x, axis=-1, keepdims=True) + RMS_EPS) * g_ref[...]
    o_ref[...] = _dot(h.astype(BF16), w_ref[...])


def _inproj(x2d, g, w, tm):
    n = x2d.shape[0]
    return pl.pallas_call(
        _inproj_kernel,
        out_shape=jax.ShapeDtypeStruct((n, PROJ_COLS), F32),
        grid=(n // tm,),
        in_specs=[
            pl.BlockSpec((tm, D_MODEL), lambda i: (i, 0)),
            pl.BlockSpec((1, D_MODEL), lambda i: (0, 0)),
            pl.BlockSpec((D_MODEL, PROJ_COLS), lambda i: (0, 0)),
        ],
        out_specs=pl.BlockSpec((tm, PROJ_COLS), lambda i: (i, 0)),
        compiler_params=pltpu.CompilerParams(
            dimension_semantics=("arbitrary",), vmem_limit_bytes=VMEM_LIMIT),
        name="inproj",
    )(x2d, g, w)


def _memkv_kernel(m_ref, g_ref, wk_ref, wv_ref, k_ref, v_ref):
    x = m_ref[...]
    h = (x * lax.rsqrt(jnp.mean(x * x, axis=-1, keepdims=True) + RMS_EPS) * g_ref[...]).astype(BF16)
    k_ref[...] = _dot(h, wk_ref[...]).astype(BF16)
    v_ref[...] = _dot(h, wv_ref[...]).astype(BF16)


def _memkv(mem2d, g, wk, wv, tm):
    n = mem2d.shape[0]
    full = pl.BlockSpec((D_MODEL, D_MODEL), lambda i: (0, 0))
    tile = pl.BlockSpec((tm, D_MODEL), lambda i: (i, 0))
    return pl.pallas_call(
        _memkv_kernel,
        out_shape=(jax.ShapeDtypeStruct((n, D_MODEL), BF16),) * 2,
        grid=(n // tm,),
        in_specs=[tile, pl.BlockSpec((1, D_MODEL), lambda i: (0, 0)), full, full],
        out_specs=(tile, tile),
        compiler_params=pltpu.CompilerParams(
            dimension_semantics=("arbitrary",), vmem_limit_bytes=VMEM_LIMIT),
        name="memkv",
    )(mem2d, g, wk, wv)


def _decay_exponents(b, lf):
    w = b.shape[-1]
    groups = CHUNK // SUB
    b3 = b.reshape(groups, SUB, w)
    row = lax.broadcasted_iota(jnp.int32, (groups, SUB, w), 1)
    levels = []
    for m in (32, 16, 8):
        gm = m // SUB
        parts = []
        for blk in range(groups // (2 * gm)):
            g0 = blk * 2 * gm
            ref = b3[g0 + gm - 1:g0 + gm, SUB - 1:SUB, :]
            parts.append(ref - b3[g0:g0 + gm])
            parts.append(b3[g0 + gm:g0 + 2 * gm] - ref)
        levels.append(jnp.concatenate(parts, axis=0))
    d4 = b3 - b3[:, 3:4, :]
    levels.append(jnp.where(row >= 4, d4, -d4))
    d2 = b3 - jnp.where(row < 4, b3[:, 1:2, :], b3[:, 5:6, :])
    levels.append(jnp.where((row & 3) >= 2, d2, -d2))
    levels.append(jnp.where((row & 1) == 1, lf.reshape(groups, SUB, w), 0.0))
    e_kstate = b3[groups - 1:groups, SUB - 1:SUB, :] - b3
    return [e.reshape(CHUNK, w) for e in levels], e_kstate.reshape(CHUNK, w)


def _pair_masks():
    i = lax.broadcasted_iota(jnp.int32, (CHUNK, LANES), 0)
    j = lax.broadcasted_iota(jnp.int32, (CHUNK, LANES), 1) & (CHUNK - 1)
    x = jnp.bitwise_xor(i, j)
    below = i > j
    masks = []
    for lvl in range(N_LEVELS):
        top_bit = N_LEVELS - 1 - lvl
        masks.append(jnp.logical_and(below, jnp.right_shift(x, top_bit) == 1))
    return i == j, masks


def _block_rows(t0, t1):
    z = jnp.zeros_like(t0)
    return jnp.concatenate([jnp.concatenate([t0, z], axis=1), jnp.concatenate([z, t1], axis=1)], axis=0)


def _pair_scores(q, k, levels, diag, masks, key_blocks):
    scores = jnp.where(diag, _dot_nt(q.astype(BF16), key_blocks(k)), 0.0)
    for lvl in range(N_LEVELS):
        ee = jnp.exp(levels[lvl])
        p = _dot_nt((q * ee).astype(BF16), key_blocks(k * ee))
        scores = jnp.where(masks[lvl], p, scores)
    return scores


def _mixer_kernel(proj_ref, tri_ref, lb_logit_ref, hnorm_ref, gup_ref, gbias_ref, gnorm_ref,
                  convw_ref, convb_ref, wq_ref, wk_ref, wv_ref, sbias_ref, skip_ref, mnorm_ref,
                  out_ref,
                  hst_ref, gst_ref, mc_ref, mn_ref, mm_ref, conv_ref, *, layer):
    @pl.when(pl.program_id(1) == 0)
    def _():
        hst_ref[...] = jnp.zeros_like(hst_ref)
        gst_ref[...] = jnp.zeros_like(gst_ref)
        mc_ref[...] = jnp.zeros_like(mc_ref)
        mn_ref[...] = jnp.zeros_like(mn_ref)
        mm_ref[...] = jnp.zeros_like(mm_ref)
        conv_ref[0:SUB, :] = jnp.zeros((SUB, N_HEADS * MLSTM_DH), F32)

    tri = tri_ref[...]
    diag, masks = _pair_masks()
    lane = lax.broadcasted_iota(jnp.int32, (CHUNK, LANES), 1)
    low_half = lane < GLA_DK

    logits = lb_logit_ref[...]
    ex = jnp.exp(logits - jnp.max(logits, axis=0, keepdims=True))
    soft = ex / jnp.sum(ex, axis=0, keepdims=True)
    c0 = soft[0:1, :]
    cl = c0
    for r in range(1, layer + 1):
        cl = cl + soft[r:r + 1, :]
    lb = cl - c0

    small = proj_ref[:, C_SMALL:C_SMALL + LANES]
    gla_pre = _dot(small.astype(BF16), gup_ref[...]) + gbias_ref[...]

    sig_f = _sigmoid(proj_ref[:, C_AF:C_AF + N_HEADS * HGRN_D])
    f_gate = lb + (1.0 - lb) * sig_f
    h_lf = jnp.maximum(jnp.log(f_gate), LOG_ZERO)
    h_k = (1.0 - lb) * (1.0 - sig_f)

    sb = small + sbias_ref[...]
    is_f = jnp.logical_and(lane >= L_MF, lane < L_MF + N_HEADS)
    is_i = jnp.logical_and(lane >= L_MI, lane < L_MI + N_HEADS)
    m_lf = jnp.where(is_f, _log_sigmoid(sb), 0.0)

    b_hm = _cumsum_rows(tri, jnp.concatenate([h_lf, m_lf], axis=1))
    h_b = b_hm[:, :N_HEADS * HGRN_D]
    m_b = b_hm[:, N_HEADS * HGRN_D:]

    m_u = proj_ref[:, C_MU:C_MU + N_HEADS * MLSTM_DH]
    conv_ref[SUB:SUB + CHUNK, :] = m_u
    acc = convb_ref[...]
    for tap in range(MLSTM_CONV):
        off = SUB - (MLSTM_CONV - 1) + tap
        acc = acc + conv_ref[off:off + CHUNK, :] * convw_ref[tap:tap + 1, :]
    conv_ref[0:SUB, :] = m_u[CHUNK - SUB:CHUNK, :]
    conv = _silu(acc)
    m_q, m_k, m_v = [], [], []
    for h in range(N_HEADS):
        sl = slice(h * MLSTM_DH, (h + 1) * MLSTM_DH)
        cb = conv[:, sl].astype(BF16)
        m_q.append(_dot(cb, wq_ref[h]))
        m_k.append(_dot(cb, wk_ref[h]) * (MLSTM_DH ** -0.5))
        m_v.append(_dot(m_u[:, sl].astype(BF16), wv_ref[h]).astype(BF16))

    z = jnp.where(is_i, sb, m_b)
    eye = (lax.broadcasted_iota(jnp.int32, (LANES, LANES), 0)
           == lax.broadcasted_iota(jnp.int32, (LANES, LANES), 1)).astype(BF16)
    zh, zm, zl = _split3(z)
    z_t = _dot_nt(eye, zh) + _dot_nt(eye, zm) + _dot_nt(eye, zl)

    g_lf = _log_sigmoid(gla_pre) * (1.0 / GLA_GATE_NORMALIZER)
    g_b = _cumsum_rows(tri, g_lf)

    h_q = _silu(proj_ref[:, C_AQ:C_AQ + N_HEADS * HGRN_D])
    h_scores, h_vb = [], []
    for pair in range(N_HEADS // 2):
        psl = slice(pair * 2 * HGRN_D, (pair + 1) * 2 * HGRN_D)
        q, k, b = h_q[:, psl], h_k[:, psl], h_b[:, psl]
        levels, e_k = _decay_exponents(b, h_lf[:, psl])
        q_inter = (q * jnp.exp(b)).astype(BF16)
        k_state = (k * jnp.exp(e_k)).astype(BF16)
        decay = jnp.exp(b[CHUNK - 1:CHUNK, :])
        vb = proj_ref[:, C_AI + pair * 2 * HGRN_D:C_AI + (pair + 1) * 2 * HGRN_D].astype(BF16)
        o_inter = []
        for hp in range(2):
            h = 2 * pair + hp
            hsl = slice(hp * HGRN_D, (hp + 1) * HGRN_D)
            st = hst_ref[h]
            o_inter.append(_dot_nt(q_inter[:, hsl], st.astype(BF16)))
            hst_ref[h] = decay[:, hsl] * st + _dot_tn(vb[:, hsl], k_state[:, hsl])
        h_scores.append((jnp.concatenate(o_inter, axis=1),
                         _pair_scores(q, k, levels, diag, masks,
                                      lambda t: _block_rows(t[:, :HGRN_D].astype(BF16),
                                                            t[:, HGRN_D:].astype(BF16)))))
        h_vb.append(vb)

    ci = lax.broadcasted_iota(jnp.int32, (CHUNK, CHUNK), 0)
    cj = lax.broadcasted_iota(jnp.int32, (CHUNK, CHUNK), 1)
    causal = ci >= cj
    m_parts = []
    for h in range(N_HEADS):
        q, k, vb = m_q[h], m_k[h], m_v[h]
        qb, kb = q.astype(BF16), k.astype(BF16)
        bc = z[:, L_MF + h:L_MF + h + 1]
        ic = z[:, L_MI + h:L_MI + h + 1]
        br = z_t[L_MF + h:L_MF + h + 1, :]
        ir = z_t[L_MI + h:L_MI + h + 1, :]
        m_prev = mm_ref[h:h + 1, 0:1]
        c_prev = mc_ref[h]
        n_prev = mn_ref[h:h + 1, :]

        log_d = jnp.where(causal, bc - br + ir, -jnp.inf)
        m_inter = bc + m_prev
        m_i = jnp.maximum(m_inter, jnp.max(log_d, axis=-1, keepdims=True))
        w_inter = jnp.exp(m_inter - m_i)
        s = _dot_nt(qb, kb) * jnp.exp(log_d - m_i)
        num_inter = w_inter * _dot(qb, c_prev.astype(BF16))
        den = (w_inter * jnp.sum(q * n_prev, axis=-1, keepdims=True)
               + jnp.sum(s, axis=-1, keepdims=True))
        scale = 1.0 / jnp.maximum(jnp.abs(den), jnp.exp(-m_i))

        b_last = bc[CHUNK - 1:CHUNK, :]
        log_w = b_last - bc + ic
        m_new = jnp.maximum(b_last + m_prev, jnp.max(log_w, axis=0, keepdims=True))
        w_prev = jnp.exp(b_last + m_prev - m_new)
        kw = jnp.exp(log_w - m_new) * k
        mc_ref[h] = w_prev * c_prev + _dot_tn(kw.astype(BF16), vb)
        mn_ref[h:h + 1, :] = w_prev * n_prev + jnp.sum(kw, axis=0, keepdims=True)
        mm_ref[h:h + 1, :] = jnp.broadcast_to(m_new, (1, LANES))
        m_parts.append((num_inter, s.astype(BF16), scale))

    g_scores, g_vb = [], []
    for pair in range(N_HEADS // 2):
        psl = slice(pair * LANES, (pair + 1) * LANES)
        q = proj_ref[:, C_GQ + pair * LANES:C_GQ + (pair + 1) * LANES] * (GLA_DK ** -0.5)
        k = proj_ref[:, C_GK + pair * LANES:C_GK + (pair + 1) * LANES]
        b = g_b[:, psl]
        levels, e_k = _decay_exponents(b, g_lf[:, psl])
        q_inter = q * jnp.exp(b)
        k_state = (k * jnp.exp(e_k)).astype(BF16)
        decay = jnp.exp(b[CHUNK - 1:CHUNK, :])
        vb = proj_ref[:, C_GV + pair * 2 * GLA_DV:C_GV + (pair + 1) * 2 * GLA_DV].astype(BF16)
        st = gst_ref[pair]
        stb = st.astype(BF16)
        o_inter = [_dot_nt(jnp.where(low_half, q_inter, 0.0).astype(BF16), stb),
                   _dot_nt(jnp.where(low_half, 0.0, q_inter).astype(BF16), stb)]
        lane_st = lax.broadcasted_iota(jnp.int32, (GLA_DV, LANES), 1)
        gst_ref[pair] = decay * st + jnp.where(lane_st < GLA_DK,
                                               _dot_tn(vb[:, :GLA_DV], k_state),
                                               _dot_tn(vb[:, GLA_DV:], k_state))
        g_scores.append((jnp.concatenate(o_inter, axis=1),
                         _pair_scores(q, k, levels, diag, masks,
                                      lambda t: jnp.concatenate(
                                          [jnp.where(low_half, t, 0.0), jnp.where(low_half, 0.0, t)],
                                          axis=0).astype(BF16))))
        g_vb.append(vb)

    for pair in range(N_HEADS // 2):
        o_inter, scores = h_scores[pair]
        vb = h_vb[pair]
        o = o_inter + _dot(scores.astype(BF16), _block_rows(vb[:, :HGRN_D], vb[:, HGRN_D:]))
        for hp in range(2):
            h = 2 * pair + hp
            oh = o[:, hp * HGRN_D:(hp + 1) * HGRN_D]
            y = oh * lax.rsqrt(jnp.mean(oh * oh, axis=-1, keepdims=True) + RMS_EPS) * hnorm_ref[...]
            y = y * _silu(proj_ref[:, C_AZ + h * HGRN_D:C_AZ + (h + 1) * HGRN_D])
            out_ref[:, h * 128:(h + 1) * 128] = y.astype(out_ref.dtype)

    for h in range(N_HEADS):
        sl = slice(h * MLSTM_DH, (h + 1) * MLSTM_DH)
        num_inter, sbf, scale = m_parts[h]
        hh = (num_inter + _dot(sbf, m_v[h])) * scale
        mu = jnp.mean(hh, axis=-1, keepdims=True)
        hc = hh - mu
        y = hc * lax.rsqrt(jnp.mean(hc * hc, axis=-1, keepdims=True) + RMS_EPS) * mnorm_ref[:, sl]
        y = (y + skip_ref[:, sl] * conv[:, sl]) * _silu(
            proj_ref[:, C_MZ + h * MLSTM_DH:C_MZ + (h + 1) * MLSTM_DH])
        out_ref[:, 1024 + h * 128:1024 + (h + 1) * 128] = y.astype(out_ref.dtype)

    for pair in range(N_HEADS // 2):
        o_inter, scores = g_scores[pair]
        vb = g_vb[pair]
        o = o_inter + _dot(scores.astype(BF16), _block_rows(vb[:, :GLA_DV], vb[:, GLA_DV:]))
        for hp in range(2):
            h = 2 * pair + hp
            oh = o[:, hp * GLA_DV:(hp + 1) * GLA_DV]
            y = oh * lax.rsqrt(jnp.mean(oh * oh, axis=-1, keepdims=True) + RMS_EPS) * gnorm_ref[...]
            y = y * _silu(proj_ref[:, C_GZ + h * GLA_DV:C_GZ + (h + 1) * GLA_DV])
            out_ref[:, 512 + h * 128:512 + (h + 1) * 128] = y.astype(out_ref.dtype)


def _mixers(proj, tri, lb_logits, hnorm, gup, gbias, gnorm, convw, convb, wq, wk, wv,
            sbias, skip, mnorm, layer):
    b, s, _ = proj.shape
    depth = lb_logits.shape[0]

    def whole(shape):
        return pl.BlockSpec(shape, lambda i, j: (0,) * len(shape))

    return pl.pallas_call(
        functools.partial(_mixer_kernel, layer=layer),
        out_shape=jax.ShapeDtypeStruct((b, s, MIX_WIDTH), BF16),
        grid=(b, s // CHUNK),
        in_specs=[
            pl.BlockSpec((None, CHUNK, PROJ_COLS), lambda i, j: (i, j, 0)),
            whole((CHUNK, CHUNK)),
            whole((depth, N_HEADS * HGRN_D)),
            whole((1, HGRN_D)),
            whole((LANES, N_HEADS * GLA_DK)),
            whole((1, N_HEADS * GLA_DK)),
            whole((1, GLA_DV)),
            whole((MLSTM_CONV, N_HEADS * MLSTM_DH)),
            whole((1, N_HEADS * MLSTM_DH)),
            whole((N_HEADS, MLSTM_DH, MLSTM_DH)),
            whole((N_HEADS, MLSTM_DH, MLSTM_DH)),
            whole((N_HEADS, MLSTM_DH, MLSTM_DH)),
            whole((1, LANES)),
            whole((1, N_HEADS * MLSTM_DH)),
            whole((1, N_HEADS * MLSTM_DH)),
        ],
        out_specs=pl.BlockSpec((None, CHUNK, MIX_WIDTH), lambda i, j: (i, j, 0)),
        scratch_shapes=[
            pltpu.VMEM((N_HEADS, HGRN_D, HGRN_D), F32),
            pltpu.VMEM((N_HEADS // 2, GLA_DV, LANES), F32),
            pltpu.VMEM((N_HEADS, MLSTM_DH, MLSTM_DH), F32),
            pltpu.VMEM((SUB, MLSTM_DH), F32),
            pltpu.VMEM((SUB, LANES), F32),
            pltpu.VMEM((SUB + CHUNK, N_HEADS * MLSTM_DH), F32),
        ],
        compiler_params=pltpu.CompilerParams(
            dimension_semantics=("arbitrary", "arbitrary"), vmem_limit_bytes=VMEM_LIMIT),
        name="mixers",
    )(proj, tri, lb_logits, hnorm, gup, gbias, gnorm, convw, convb, wq, wk, wv, sbias, skip, mnorm)


def _outattn_kernel(mix_ref, x_ref, wout_ref, gx_ref, wq_ref, k_ref, v_ref, wo_ref, gf_ref,
                    o_ref, *, final_norm):
    x1 = x_ref[...] + _dot(mix_ref[...], wout_ref[...])
    h = (x1 * lax.rsqrt(jnp.mean(x1 * x1, axis=-1, keepdims=True) + RMS_EPS) * gx_ref[...]).astype(BF16)
    q = _dot(h, wq_ref[...])
    outs = []
    for hd in range(N_HEADS):
        sl = slice(hd * XA_HEAD_DIM, (hd + 1) * XA_HEAD_DIM)
        s = _dot_nt(q[:, sl].astype(BF16), k_ref[:, sl]) * (XA_HEAD_DIM ** -0.5)
        e = jnp.exp(s - jnp.max(s, axis=-1, keepdims=True))
        p = e / jnp.sum(e, axis=-1, keepdims=True)
        outs.append(_dot(p.astype(BF16), v_ref[:, sl]).astype(BF16))
    o = jnp.concatenate(outs, axis=-1)
    x2 = x1 + _dot(o, wo_ref[...])
    if final_norm:
        x2 = x2 * lax.rsqrt(jnp.mean(x2 * x2, axis=-1, keepdims=True) + RMS_EPS) * gf_ref[...]
    o_ref[...] = x2


def _outattn(mix, x, wout, gx, wq, kmem, vmem, wo, gf, tm, final_norm):
    b, s, _ = x.shape
    m = kmem.shape[1]

    def whole(shape):
        return pl.BlockSpec(shape, lambda i, j: (0,) * len(shape))

    return pl.pallas_call(
        functools.partial(_outattn_kernel, final_norm=final_norm),
        out_shape=jax.ShapeDtypeStruct((b, s, D_MODEL), F32),
        grid=(b, s // tm),
        in_specs=[
            pl.BlockSpec((None, tm, MIX_WIDTH), lambda i, j: (i, j, 0)),
            pl.BlockSpec((None, tm, D_MODEL), lambda i, j: (i, j, 0)),
            whole((MIX_WIDTH, D_MODEL)),
            whole((1, D_MODEL)),
            whole((D_MODEL, D_MODEL)),
            pl.BlockSpec((None, m, D_MODEL), lambda i, j: (i, 0, 0)),
            pl.BlockSpec((None, m, D_MODEL), lambda i, j: (i, 0, 0)),
            whole((D_MODEL, D_MODEL)),
            whole((1, D_MODEL)),
        ],
        out_specs=pl.BlockSpec((None, tm, D_MODEL), lambda i, j: (i, j, 0)),
        compiler_params=pltpu.CompilerParams(
            dimension_semantics=("arbitrary", "arbitrary"), vmem_limit_bytes=VMEM_LIMIT),
        name="outattn",
    )(mix, x, wout, gx, wq, kmem, vmem, wo, gf)


def _pack_w_in(w):
    ga0 = 2048 + 2 * N_HEADS * GLA_DK + N_HEADS * GLA_DV
    ga1 = ga0 + GLA_RANK
    gate0 = ga1 + N_HEADS * GLA_DV + 2 * N_HEADS * MLSTM_DH
    pad = jnp.zeros((w.shape[0], LANES - GLA_RANK - 2 * N_HEADS), w.dtype)
    return jnp.concatenate([w[:, :ga0], w[:, ga1:gate0], w[:, ga0:ga1], w[:, gate0:], pad],
                           axis=1).astype(BF16)


def _block_diag_dense(w):
    blk = w.shape[1]
    per = MLSTM_DH // blk
    wg = w.reshape(N_HEADS, per, blk, blk)
    eye = jnp.eye(per, dtype=w.dtype)
    dense = wg[:, :, :, None, :] * eye[None, :, None, :, None]
    return dense.reshape(N_HEADS, MLSTM_DH, MLSTM_DH).astype(BF16)


def kernel(x, mem, norm_mix, w_in, hgrn_lb_logits, hgrn_norm, gla_gate_up, gla_gate_bias, gla_norm,
           mlstm_conv_w, mlstm_conv_b, mlstm_wq, mlstm_wk, mlstm_wv, mlstm_igate_bias,
           mlstm_fgate_bias, mlstm_skip, mlstm_norm, w_out, norm_xattn, norm_mem, xa_wq, xa_wk,
           xa_wv, xa_wo, norm_final):
    b, s, d = x.shape
    m = mem.shape[1]
    depth = w_in.shape[0]
    tri = jnp.asarray(np.tril(np.ones((CHUNK, CHUNK), np.float32)), dtype=BF16)
    mem2d = mem.reshape(b * m, d)
    zpad = jnp.zeros((LANES - GLA_RANK - 2 * N_HEADS,), F32)

    for l in range(depth):
        proj = _inproj(x.reshape(b * s, d), norm_mix[l][None, :], _pack_w_in(w_in[l]), tm=256)
        gup = jnp.concatenate(
            [gla_gate_up[l], jnp.zeros((LANES - GLA_RANK, N_HEADS * GLA_DK), F32)], axis=0).astype(BF16)
        sbias = jnp.concatenate(
            [jnp.zeros((GLA_RANK,), F32), mlstm_igate_bias[l], mlstm_fgate_bias[l], zpad])[None, :]
        mix = _mixers(
            proj.reshape(b, s, PROJ_COLS), tri, hgrn_lb_logits, hgrn_norm[l][None, :], gup,
            gla_gate_bias[l][None, :], gla_norm[l][None, :], mlstm_conv_w[l],
            mlstm_conv_b[l][None, :], _block_diag_dense(mlstm_wq[l]), _block_diag_dense(mlstm_wk[l]),
            _block_diag_dense(mlstm_wv[l]), sbias, mlstm_skip[l][None, :], mlstm_norm[l][None, :],
            layer=l)
        kmem, vmem = _memkv(mem2d, norm_mem[l][None, :], xa_wk[l].astype(BF16),
                            xa_wv[l].astype(BF16), tm=256)
        x = _outattn(mix, x, w_out[l].astype(BF16), norm_xattn[l][None, :], xa_wq[l].astype(BF16),
                     kmem.reshape(b, m, d), vmem.reshape(b, m, d), xa_wo[l].astype(BF16),
                     norm_final[None, :], tm=min(512, s), final_norm=(l == depth - 1))
    return x
```

```python
import functools
import math

import numpy as np
import jax
import jax.numpy as jnp
from jax import lax
from jax.experimental import pallas as pl
from jax.experimental.pallas import tpu as pltpu

F32 = jnp.float32
BF16 = jnp.bfloat16

RMS_EPS = 1e-6
D_MODEL = 1024
CHUNK = 64
STEP_CHUNKS = 4
SUB = 8
LANES = 128
N_HEADS = 4
HGRN_D = 128
GLA_DK = 64
GLA_DV = 128
GLA_RANK = 16
GLA_GATE_NORMALIZER = 16.0
MLSTM_DH = 128
MLSTM_CONV = 4
XA_HEAD_DIM = D_MODEL // N_HEADS
MIX_WIDTH = 3 * N_HEADS * 128

C_AQ, C_AF, C_AI, C_AZ = 0, 512, 1024, 1536
C_GQ, C_GK, C_GV, C_GZ = 2048, 2304, 2560, 3072
C_MU, C_MZ = 3584, 4096
C_SMALL = 4608
PROJ_COLS = C_SMALL + LANES
L_MI = GLA_RANK
L_MF = GLA_RANK + N_HEADS

N_LEVELS = 6
LOG_ZERO = -1e30
LOG2E = math.log2(math.e)
VMEM_LIMIT = 48 * 1024 * 1024


def _sigmoid(x):
    return 1.0 / (1.0 + jnp.exp(-x))


def _silu(x):
    return x * _sigmoid(x)


def _log_sigmoid(x):
    return jnp.minimum(x, 0.0) - jnp.log(1.0 + jnp.exp(-jnp.abs(x)))


def _dot(a, b):
    return jnp.dot(a, b, preferred_element_type=F32)


def _dot_nt(a, b):
    return lax.dot_general(a, b, (((1,), (1,)), ((), ())), preferred_element_type=F32)


def _dot_tn(a, b):
    return lax.dot_general(a, b, (((0,), (0,)), ((), ())), preferred_element_type=F32)


def _split3(x):
    hi = x.astype(BF16)
    r1 = x - hi.astype(F32)
    mid = r1.astype(BF16)
    lo = (r1 - mid.astype(F32)).astype(BF16)
    return hi, mid, lo


def _cumsum_rows(tri_bf16, x):
    hi = x.astype(BF16)
    lo = (x - hi.astype(F32)).astype(BF16)
    return _dot(tri_bf16, hi) + _dot(tri_bf16, lo)


def _inproj_kernel(x_ref, g_ref, w_ref, o_ref):
    x = x_ref[...]
    h = x * lax.rsqrt(jnp.mean(x * x, axis=-1, keepdims=True) + RMS_EPS) * g_ref[...]
    o_ref[...] = _dot(h.astype(BF16), w_ref[...])


def _inproj(x2d, g, w, tm):
    n = x2d.shape[0]
    return pl.pallas_call(
        _inproj_kernel,
        out_shape=jax.ShapeDtypeStruct((n, PROJ_COLS), F32),
        grid=(n // tm,),
        in_specs=[
            pl.BlockSpec((tm, D_MODEL), lambda i: (i, 0)),
            pl.BlockSpec((1, D_MODEL), lambda i: (0, 0)),
            pl.BlockSpec((D_MODEL, PROJ_COLS), lambda i: (0, 0)),
        ],
        out_specs=pl.BlockSpec((tm, PROJ_COLS), lambda i: (i, 0)),
        compiler_params=pltpu.CompilerParams(
            dimension_semantics=("arbitrary",), vmem_limit_bytes=VMEM_LIMIT),
        name="inproj",
    )(x2d, g, w)


def _memkv_kernel(m_ref, g_ref, wk_ref, wv_ref, k_ref, v_ref):
    x = m_ref[...]
    h = (x * lax.rsqrt(jnp.mean(x * x, axis=-1, keepdims=True) + RMS_EPS) * g_ref[...]).astype(BF16)
    k_ref[...] = _dot(h, wk_ref[...]).astype(BF16)
    v_ref[...] = _dot(h, wv_ref[...]).astype(BF16)


def _memkv(mem2d, g, wk, wv, tm):
    n = mem2d.shape[0]
    full = pl.BlockSpec((D_MODEL, D_MODEL), lambda i: (0, 0))
    tile = pl.BlockSpec((tm, D_MODEL), lambda i: (i, 0))
    return pl.pallas_call(
        _memkv_kernel,
        out_shape=(jax.ShapeDtypeStruct((n, D_MODEL), BF16),) * 2,
        grid=(n // tm,),
        in_specs=[tile, pl.BlockSpec((1, D_MODEL), lambda i: (0, 0)), full, full],
        out_specs=(tile, tile),
        compiler_params=pltpu.CompilerParams(
            dimension_semantics=("arbitrary",), vmem_limit_bytes=VMEM_LIMIT),
        name="memkv",
    )(mem2d, g, wk, wv)


def _decay_exponents(b, lf):
    w = b.shape[-1]
    groups = CHUNK // SUB
    b3 = b.reshape(groups, SUB, w)
    row = lax.broadcasted_iota(jnp.int32, (groups, SUB, w), 1)
    levels = []
    for m in (32, 16, 8):
        gm = m // SUB
        parts = []
        for blk in range(groups // (2 * gm)):
            g0 = blk * 2 * gm
            ref = b3[g0 + gm - 1:g0 + gm, SUB - 1:SUB, :]
            parts.append(ref - b3[g0:g0 + gm])
            parts.append(b3[g0 + gm:g0 + 2 * gm] - ref)
        levels.append(jnp.concatenate(parts, axis=0))
    d4 = b3 - b3[:, 3:4, :]
    levels.append(jnp.where(row >= 4, d4, -d4))
    d2 = b3 - jnp.where(row < 4, b3[:, 1:2, :], b3[:, 5:6, :])
    levels.append(jnp.where((row & 3) >= 2, d2, -d2))
    levels.append(jnp.where((row & 1) == 1, lf.reshape(groups, SUB, w), 0.0))
    e_kstate = b3[groups - 1:groups, SUB - 1:SUB, :] - b3
    return [e.reshape(CHUNK, w) for e in levels], e_kstate.reshape(CHUNK, w)


def _pair_masks():
    i = lax.broadcasted_iota(jnp.int32, (CHUNK, LANES), 0)
    j = lax.broadcasted_iota(jnp.int32, (CHUNK, LANES), 1) & (CHUNK - 1)
    x = jnp.bitwise_xor(i, j)
    below = i > j
    masks = []
    for lvl in range(N_LEVELS):
        top_bit = N_LEVELS - 1 - lvl
        masks.append(jnp.logical_and(below, jnp.right_shift(x, top_bit) == 1))
    return i == j, masks


def _block_rows(t0, t1):
    z = jnp.zeros_like(t0)
    return jnp.concatenate([jnp.concatenate([t0, z], axis=1), jnp.concatenate([z, t1], axis=1)], axis=0)


def _pair_scores(q16, levels, diag, masks, key_blocks):
    scores = jnp.where(diag, _dot_nt(q16, key_blocks(None)), 0.0)
    for lvl in range(N_LEVELS):
        ee = jnp.exp2(levels[lvl]).astype(BF16)
        p = _dot_nt(q16 * ee, key_blocks(ee))
        scores = jnp.where(masks[lvl], p, scores)
    return scores


def _mixer_kernel(proj_ref, tri_ref, lb_logit_ref, hnorm_ref, gup_ref, gbias_ref, gnorm_ref,
                  convw_ref, convb_ref, wq_ref, wk_ref, wv_ref, sbias_ref, skip_ref, mnorm_ref,
                  out_ref,
                  hst_ref, gst_ref, mc_ref, mn_ref, mm_ref, conv_ref, *, layer):
    @pl.when(pl.program_id(1) == 0)
    def _():
        hst_ref[...] = jnp.zeros_like(hst_ref)
        gst_ref[...] = jnp.zeros_like(gst_ref)
        mc_ref[...] = jnp.zeros_like(mc_ref)
        mn_ref[...] = jnp.zeros_like(mn_ref)
        mm_ref[...] = jnp.zeros_like(mm_ref)
        conv_ref[0:SUB, :] = jnp.zeros((SUB, N_HEADS * MLSTM_DH), F32)

    tri = tri_ref[...]
    diag, masks = _pair_masks()
    lane = lax.broadcasted_iota(jnp.int32, (CHUNK, LANES), 1)
    low_half = lane < GLA_DK
    is_f = jnp.logical_and(lane >= L_MF, lane < L_MF + N_HEADS)
    is_i = jnp.logical_and(lane >= L_MI, lane < L_MI + N_HEADS)
    lane_st = lax.broadcasted_iota(jnp.int32, (GLA_DV, LANES), 1)
    eye = (lax.broadcasted_iota(jnp.int32, (LANES, LANES), 0)
           == lax.broadcasted_iota(jnp.int32, (LANES, LANES), 1)).astype(BF16)
    ci = lax.broadcasted_iota(jnp.int32, (CHUNK, CHUNK), 0)
    cj = lax.broadcasted_iota(jnp.int32, (CHUNK, CHUNK), 1)
    causal = ci >= cj

    logits = lb_logit_ref[...]
    ex = jnp.exp(logits - jnp.max(logits, axis=0, keepdims=True))
    soft = ex / jnp.sum(ex, axis=0, keepdims=True)
    c0 = soft[0:1, :]
    cl = c0
    for r in range(1, layer + 1):
        cl = cl + soft[r:r + 1, :]
    lb = cl - c0
    one_m_lb = 1.0 - lb

    def chunk(r0):
        rows = slice(r0, r0 + CHUNK)

        small = proj_ref[rows, C_SMALL:C_SMALL + LANES]
        gla_pre = _dot(small.astype(BF16), gup_ref[...]) + gbias_ref[...]

        sig_f = _sigmoid(proj_ref[rows, C_AF:C_AF + N_HEADS * HGRN_D])
        h_lf = jnp.maximum(jnp.log(lb + one_m_lb * sig_f) * LOG2E, LOG_ZERO)
        h_k = one_m_lb * (1.0 - sig_f)

        sb = small + sbias_ref[...]
        m_lf = jnp.where(is_f, _log_sigmoid(sb), 0.0)

        b_hm = _cumsum_rows(tri, jnp.concatenate([h_lf, m_lf], axis=1))
        h_b = b_hm[:, :N_HEADS * HGRN_D]
        m_b = b_hm[:, N_HEADS * HGRN_D:]

        m_u = proj_ref[rows, C_MU:C_MU + N_HEADS * MLSTM_DH]
        conv_ref[SUB:SUB + CHUNK, :] = m_u
        acc = convb_ref[...]
        for tap in range(MLSTM_CONV):
            off = SUB - (MLSTM_CONV - 1) + tap
            acc = acc + conv_ref[off:off + CHUNK, :] * convw_ref[tap:tap + 1, :]
        conv_ref[0:SUB, :] = m_u[CHUNK - SUB:CHUNK, :]
        conv = _silu(acc)
        m_q, m_k, m_v = [], [], []
        for h in range(N_HEADS):
            sl = slice(h * MLSTM_DH, (h + 1) * MLSTM_DH)
            cb = conv[:, sl].astype(BF16)
            m_q.append(_dot(cb, wq_ref[h]))
            m_k.append(_dot(cb, wk_ref[h]) * (MLSTM_DH ** -0.5))
            m_v.append(_dot(m_u[:, sl].astype(BF16), wv_ref[h]).astype(BF16))

        z = jnp.where(is_i, sb, m_b)
        zh, zm, zl = _split3(z)
        z_t = _dot_nt(eye, zh) + _dot_nt(eye, zm) + _dot_nt(eye, zl)

        g_lf = _log_sigmoid(gla_pre) * (LOG2E / GLA_GATE_NORMALIZER)
        g_b = _cumsum_rows(tri, g_lf)

        h_q = _silu(proj_ref[rows, C_AQ:C_AQ + N_HEADS * HGRN_D])
        h_scores, h_vb = [], []
        for pair in range(N_HEADS // 2):
            psl = slice(pair * 2 * HGRN_D, (pair + 1) * 2 * HGRN_D)
            q, k, b = h_q[:, psl], h_k[:, psl], h_b[:, psl]
            levels, e_k = _decay_exponents(b, h_lf[:, psl])
            q_inter = (q * jnp.exp2(b)).astype(BF16)
            k_state = (k * jnp.exp2(e_k)).astype(BF16)
            decay = jnp.exp2(b[CHUNK - 1:CHUNK, :])
            vb = proj_ref[rows, C_AI + pair * 2 * HGRN_D:C_AI + (pair + 1) * 2 * HGRN_D].astype(BF16)
            o_inter = []
            for hp in range(2):
                h = 2 * pair + hp
                hsl = slice(hp * HGRN_D, (hp + 1) * HGRN_D)
                st = hst_ref[h]
                o_inter.append(_dot_nt(q_inter[:, hsl], st.astype(BF16)))
                hst_ref[h] = decay[:, hsl] * st + _dot_tn(vb[:, hsl], k_state[:, hsl])
            k16 = k.astype(BF16)

            def h_keys(scale, k16=k16):
                t = k16 if scale is None else k16 * scale
                return _block_rows(t[:, :HGRN_D], t[:, HGRN_D:])

            h_scores.append((jnp.concatenate(o_inter, axis=1),
                             _pair_scores(q.astype(BF16), levels, diag, masks, h_keys)))
            h_vb.append(vb)

        m_parts = []
        for h in range(N_HEADS):
            q, k, vb = m_q[h], m_k[h], m_v[h]
            qb, kb = q.astype(BF16), k.astype(BF16)
            bc = z[:, L_MF + h:L_MF + h + 1]
            ic = z[:, L_MI + h:L_MI + h + 1]
            br = z_t[L_MF + h:L_MF + h + 1, :]
            ir = z_t[L_MI + h:L_MI + h + 1, :]
            m_prev = mm_ref[h:h + 1, 0:1]
            c_prev = mc_ref[h]
            n_prev = mn_ref[h:h + 1, :]

            log_d = jnp.where(causal, bc - br + ir, -jnp.inf)
            m_inter = bc + m_prev
            m_i = jnp.maximum(m_inter, jnp.max(log_d, axis=-1, keepdims=True))
            w_inter = jnp.exp(m_inter - m_i)
            s = _dot_nt(qb, kb) * jnp.exp(log_d - m_i)
            num_inter = w_inter * _dot(qb, c_prev.astype(BF16))
            den = (w_inter * jnp.sum(q * n_prev, axis=-1, keepdims=True)
                   + jnp.sum(s, axis=-1, keepdims=True))
            scale = 1.0 / jnp.maximum(jnp.abs(den), jnp.exp(-m_i))

            b_last = bc[CHUNK - 1:CHUNK, :]
            log_w = b_last - bc + ic
            m_new = jnp.maximum(b_last + m_prev, jnp.max(log_w, axis=0, keepdims=True))
            w_prev = jnp.exp(b_last + m_prev - m_new)
            kw = jnp.exp(log_w - m_new) * k
            mc_ref[h] = w_prev * c_prev + _dot_tn(kw.astype(BF16), vb)
            mn_ref[h:h + 1, :] = w_prev * n_prev + jnp.sum(kw, axis=0, keepdims=True)
            mm_ref[h:h + 1, :] = jnp.broadcast_to(m_new, (1, LANES))
            m_parts.append((num_inter, s.astype(BF16), scale))

        g_scores, g_vb = [], []
        for pair in range(N_HEADS // 2):
            psl = slice(pair * LANES, (pair + 1) * LANES)
            q = proj_ref[rows, C_GQ + pair * LANES:C_GQ + (pair + 1) * LANES] * (GLA_DK ** -0.5)
            k = proj_ref[rows, C_GK + pair * LANES:C_GK + (pair + 1) * LANES]
            b = g_b[:, psl]
            levels, e_k = _decay_exponents(b, g_lf[:, psl])
            q_inter = q * jnp.exp2(b)
            k_state = (k * jnp.exp2(e_k)).astype(BF16)
            decay = jnp.exp2(b[CHUNK - 1:CHUNK, :])
            vb = proj_ref[rows, C_GV + pair * 2 * GLA_DV:C_GV + (pair + 1) * 2 * GLA_DV].astype(BF16)
            st = gst_ref[pair]
            stb = st.astype(BF16)
            o_inter = [_dot_nt(jnp.where(low_half, q_inter, 0.0).astype(BF16), stb),
                       _dot_nt(jnp.where(low_half, 0.0, q_inter).astype(BF16), stb)]
            gst_ref[pair] = decay * st + jnp.where(lane_st < GLA_DK,
                                                   _dot_tn(vb[:, :GLA_DV], k_state),
                                                   _dot_tn(vb[:, GLA_DV:], k_state))
            k_lo = jnp.where(low_half, k, 0.0).astype(BF16)
            k_hi = jnp.where(low_half, 0.0, k).astype(BF16)

            def g_keys(scale, k_lo=k_lo, k_hi=k_hi):
                if scale is None:
                    return jnp.concatenate([k_lo, k_hi], axis=0)
                return jnp.concatenate([k_lo * scale, k_hi * scale], axis=0)

            g_scores.append((jnp.concatenate(o_inter, axis=1),
                             _pair_scores(q.astype(BF16), levels, diag, masks, g_keys)))
            g_vb.append(vb)

        for pair in range(N_HEADS // 2):
            o_inter, scores = h_scores[pair]
            vb = h_vb[pair]
            o = o_inter + _dot(scores.astype(BF16), _block_rows(vb[:, :HGRN_D], vb[:, HGRN_D:]))
            for hp in range(2):
                h = 2 * pair + hp
                oh = o[:, hp * HGRN_D:(hp + 1) * HGRN_D]
                y = oh * lax.rsqrt(jnp.mean(oh * oh, axis=-1, keepdims=True) + RMS_EPS) * hnorm_ref[...]
                y = y * _silu(proj_ref[rows, C_AZ + h * HGRN_D:C_AZ + (h + 1) * HGRN_D])
                out_ref[rows, h * 128:(h + 1) * 128] = y.astype(out_ref.dtype)

        for h in range(N_HEADS):
            sl = slice(h * MLSTM_DH, (h + 1) * MLSTM_DH)
            num_inter, sbf, scale = m_parts[h]
            hh = (num_inter + _dot(sbf, m_v[h])) * scale
            mu = jnp.mean(hh, axis=-1, keepdims=True)
            hc = hh - mu
            y = hc * lax.rsqrt(jnp.mean(hc * hc, axis=-1, keepdims=True) + RMS_EPS) * mnorm_ref[:, sl]
            y = (y + skip_ref[:, sl] * conv[:, sl]) * _silu(
                proj_ref[rows, C_MZ + h * MLSTM_DH:C_MZ + (h + 1) * MLSTM_DH])
            out_ref[rows, 1024 + h * 128:1024 + (h + 1) * 128] = y.astype(out_ref.dtype)

        for pair in range(N_HEADS // 2):
            o_inter, scores = g_scores[pair]
            vb = g_vb[pair]
            o = o_inter + _dot(scores.astype(BF16), _block_rows(vb[:, :GLA_DV], vb[:, GLA_DV:]))
            for hp in range(2):
                h = 2 * pair + hp
                oh = o[:, hp * GLA_DV:(hp + 1) * GLA_DV]
                y = oh * lax.rsqrt(jnp.mean(oh * oh, axis=-1, keepdims=True) + RMS_EPS) * gnorm_ref[...]
                y = y * _silu(proj_ref[rows, C_GZ + h * GLA_DV:C_GZ + (h + 1) * GLA_DV])
                out_ref[rows, 512 + h * 128:512 + (h + 1) * 128] = y.astype(out_ref.dtype)

    for c in range(proj_ref.shape[0] // CHUNK):
        chunk(c * CHUNK)


def _mixers(proj, tri, lb_logits, hnorm, gup, gbias, gnorm, convw, convb, wq, wk, wv,
            sbias, skip, mnorm, layer):
    b, s, _ = proj.shape
    depth = lb_logits.shape[0]
    t = CHUNK * STEP_CHUNKS

    def whole(shape):
        return pl.BlockSpec(shape, lambda i, j: (0,) * len(shape))

    return pl.pallas_call(
        functools.partial(_mixer_kernel, layer=layer),
        out_shape=jax.ShapeDtypeStruct((b, s, MIX_WIDTH), BF16),
        grid=(b, s // t),
        in_specs=[
            pl.BlockSpec((None, t, PROJ_COLS), lambda i, j: (i, j, 0)),
            whole((CHUNK, CHUNK)),
            whole((depth, N_HEADS * HGRN_D)),
            whole((1, HGRN_D)),
            whole((LANES, N_HEADS * GLA_DK)),
            whole((1, N_HEADS * GLA_DK)),
            whole((1, GLA_DV)),
            whole((MLSTM_CONV, N_HEADS * MLSTM_DH)),
            whole((1, N_HEADS * MLSTM_DH)),
            whole((N_HEADS, MLSTM_DH, MLSTM_DH)),
            whole((N_HEADS, MLSTM_DH, MLSTM_DH)),
            whole((N_HEADS, MLSTM_DH, MLSTM_DH)),
            whole((1, LANES)),
            whole((1, N_HEADS * MLSTM_DH)),
            whole((1, N_HEADS * MLSTM_DH)),
        ],
        out_specs=pl.BlockSpec((None, t, MIX_WIDTH), lambda i, j: (i, j, 0)),
        scratch_shapes=[
            pltpu.VMEM((N_HEADS, HGRN_D, HGRN_D), F32),
            pltpu.VMEM((N_HEADS // 2, GLA_DV, LANES), F32),
            pltpu.VMEM((N_HEADS, MLSTM_DH, MLSTM_DH), F32),
            pltpu.VMEM((SUB, MLSTM_DH), F32),
            pltpu.VMEM((SUB, LANES), F32),
            pltpu.VMEM((SUB + CHUNK, N_HEADS * MLSTM_DH), F32),
        ],
        compiler_params=pltpu.CompilerParams(
            dimension_semantics=("arbitrary", "arbitrary"), vmem_limit_bytes=VMEM_LIMIT),
        name="mixers",
    )(proj, tri, lb_logits, hnorm, gup, gbias, gnorm, convw, convb, wq, wk, wv, sbias, skip, mnorm)


def _outattn_kernel(mix_ref, x_ref, wout_ref, gx_ref, wq_ref, k_ref, v_ref, wo_ref, gf_ref,
                    o_ref, *, final_norm):
    x1 = x_ref[...] + _dot(mix_ref[...], wout_ref[...])
    h = (x1 * lax.rsqrt(jnp.mean(x1 * x1, axis=-1, keepdims=True) + RMS_EPS) * gx_ref[...]).astype(BF16)
    q = _dot(h, wq_ref[...])
    outs = []
    for hd in range(N_HEADS):
        sl = slice(hd * XA_HEAD_DIM, (hd + 1) * XA_HEAD_DIM)
        s = _dot_nt(q[:, sl].astype(BF16), k_ref[:, sl]) * (XA_HEAD_DIM ** -0.5)
        e = jnp.exp(s - jnp.max(s, axis=-1, keepdims=True))
        p = e / jnp.sum(e, axis=-1, keepdims=True)
        outs.append(_dot(p.astype(BF16), v_ref[:, sl]).astype(BF16))
    o = jnp.concatenate(outs, axis=-1)
    x2 = x1 + _dot(o, wo_ref[...])
    if final_norm:
        x2 = x2 * lax.rsqrt(jnp.mean(x2 * x2, axis=-1, keepdims=True) + RMS_EPS) * gf_ref[...]
    o_ref[...] = x2


def _outattn(mix, x, wout, gx, wq, kmem, vmem, wo, gf, tm, final_norm):
    b, s, _ = x.shape
    m = kmem.shape[1]

    def whole(shape):
        return pl.BlockSpec(shape, lambda i, j: (0,) * len(shape))

    return pl.pallas_call(
        functools.partial(_outattn_kernel, final_norm=final_norm),
        out_shape=jax.ShapeDtypeStruct((b, s, D_MODEL), F32),
        grid=(b, s // tm),
        in_specs=[
            pl.BlockSpec((None, tm, MIX_WIDTH), lambda i, j: (i, j, 0)),
            pl.BlockSpec((None, tm, D_MODEL), lambda i, j: (i, j, 0)),
            whole((MIX_WIDTH, D_MODEL)),
            whole((1, D_MODEL)),
            whole((D_MODEL, D_MODEL)),
            pl.BlockSpec((None, m, D_MODEL), lambda i, j: (i, 0, 0)),
            pl.BlockSpec((None, m, D_MODEL), lambda i, j: (i, 0, 0)),
            whole((D_MODEL, D_MODEL)),
            whole((1, D_MODEL)),
        ],
        out_specs=pl.BlockSpec((None, tm, D_MODEL), lambda i, j: (i, j, 0)),
        compiler_params=pltpu.CompilerParams(
            dimension_semantics=("arbitrary", "arbitrary"), vmem_limit_bytes=VMEM_LIMIT),
        name="outattn",
    )(mix, x, wout, gx, wq, kmem, vmem, wo, gf)


def _pack_w_in(w):
    ga0 = 2048 + 2 * N_HEADS * GLA_DK + N_HEADS * GLA_DV
    ga1 = ga0 + GLA_RANK
    gate0 = ga1 + N_HEADS * GLA_DV + 2 * N_HEADS * MLSTM_DH
    pad = jnp.zeros((w.shape[0], LANES - GLA_RANK - 2 * N_HEADS), w.dtype)
    return jnp.concatenate([w[:, :ga0], w[:, ga1:gate0], w[:, ga0:ga1], w[:, gate0:], pad],
                           axis=1).astype(BF16)


def _block_diag_dense(w):
    blk = w.shape[1]
    per = MLSTM_DH // blk
    wg = w.reshape(N_HEADS, per, blk, blk)
    eye = jnp.eye(per, dtype=w.dtype)
    dense = wg[:, :, :, None, :] * eye[None, :, None, :, None]
    return dense.reshape(N_HEADS, MLSTM_DH, MLSTM_DH).astype(BF16)


def kernel(x, mem, norm_mix, w_in, hgrn_lb_logits, hgrn_norm, gla_gate_up, gla_gate_bias, gla_norm,
           mlstm_conv_w, mlstm_conv_b, mlstm_wq, mlstm_wk, mlstm_wv, mlstm_igate_bias,
           mlstm_fgate_bias, mlstm_skip, mlstm_norm, w_out, norm_xattn, norm_mem, xa_wq, xa_wk,
           xa_wv, xa_wo, norm_final):
    b, s, d = x.shape
    m = mem.shape[1]
    depth = w_in.shape[0]
    tri = jnp.asarray(np.tril(np.ones((CHUNK, CHUNK), np.float32)), dtype=BF16)
    mem2d = mem.reshape(b * m, d)
    zpad = jnp.zeros((LANES - GLA_RANK - 2 * N_HEADS,), F32)

    for l in range(depth):
        proj = _inproj(x.reshape(b * s, d), norm_mix[l][None, :], _pack_w_in(w_in[l]), tm=256)
        gup = jnp.concatenate(
            [gla_gate_up[l], jnp.zeros((LANES - GLA_RANK, N_HEADS * GLA_DK), F32)], axis=0).astype(BF16)
        sbias = jnp.concatenate(
            [jnp.zeros((GLA_RANK,), F32), mlstm_igate_bias[l], mlstm_fgate_bias[l], zpad])[None, :]
        mix = _mixers(
            proj.reshape(b, s, PROJ_COLS), tri, hgrn_lb_logits, hgrn_norm[l][None, :], gup,
            gla_gate_bias[l][None, :], gla_norm[l][None, :], mlstm_conv_w[l],
            mlstm_conv_b[l][None, :], _block_diag_dense(mlstm_wq[l]), _block_diag_dense(mlstm_wk[l]),
            _block_diag_dense(mlstm_wv[l]), sbias, mlstm_skip[l][None, :], mlstm_norm[l][None, :],
            layer=l)
        kmem, vmem = _memkv(mem2d, norm_mem[l][None, :], xa_wk[l].astype(BF16),
                            xa_wv[l].astype(BF16), tm=256)
        x = _outattn(mix, x, w_out[l].astype(BF16), norm_xattn[l][None, :], xa_wq[l].astype(BF16),
                     kmem.reshape(b, m, d), vmem.reshape(b, m, d), xa_wo[l].astype(BF16),
                     norm_final[None, :], tm=min(512, s), final_norm=(l == depth - 1))
    return x
```

```python
import functools
import math

import numpy as np
import jax
import jax.numpy as jnp
from jax import lax
from jax.experimental import pallas as pl
from jax.experimental.pallas import tpu as pltpu

F32 = jnp.float32
BF16 = jnp.bfloat16

RMS_EPS = 1e-6
D_MODEL = 1024
CHUNK = 64
STEP_CHUNKS = 2
STEP_SEQS = 2
SUB = 8
LANES = 128
N_HEADS = 4
HGRN_D = 128
GLA_DK = 64
GLA_DV = 128
GLA_RANK = 16
GLA_GATE_NORMALIZER = 16.0
MLSTM_DH = 128
MLSTM_CONV = 4
XA_HEAD_DIM = D_MODEL // N_HEADS
MIX_WIDTH = 3 * N_HEADS * 128

C_AQ, C_AF, C_AI, C_AZ = 0, 512, 1024, 1536
C_GQ, C_GK, C_GV, C_GZ = 2048, 2304, 2560, 3072
C_MU, C_MZ = 3584, 4096
C_SMALL = 4608
PROJ_COLS = C_SMALL + LANES
L_MI = GLA_RANK
L_MF = GLA_RANK + N_HEADS

N_LEVELS = 6
LOG_ZERO = -1e30
LOG2E = math.log2(math.e)
VMEM_LIMIT = 48 * 1024 * 1024


def _sigmoid(x):
    return 1.0 / (1.0 + jnp.exp(-x))


def _silu(x):
    return x * _sigmoid(x)


def _log_sigmoid(x):
    return jnp.minimum(x, 0.0) - jnp.log(1.0 + jnp.exp(-jnp.abs(x)))


def _dot(a, b):
    return jnp.dot(a, b, preferred_element_type=F32)


def _dot_nt(a, b):
    return lax.dot_general(a, b, (((1,), (1,)), ((), ())), preferred_element_type=F32)


def _dot_tn(a, b):
    return lax.dot_general(a, b, (((0,), (0,)), ((), ())), preferred_element_type=F32)


def _rms_norm(x, g):
    return x * lax.rsqrt(jnp.mean(x * x, axis=-1, keepdims=True) + RMS_EPS) * g


def _split3(x):
    hi = x.astype(BF16)
    r1 = x - hi.astype(F32)
    mid = r1.astype(BF16)
    lo = (r1 - mid.astype(F32)).astype(BF16)
    return hi, mid, lo


def _cumsum_rows(tri_bf16, x):
    hi = x.astype(BF16)
    lo = (x - hi.astype(F32)).astype(BF16)
    return _dot(tri_bf16, hi) + _dot(tri_bf16, lo)


def _inproj_kernel(x_ref, g_ref, w_ref, o_ref):
    o_ref[...] = _dot(_rms_norm(x_ref[...], g_ref[...]).astype(BF16), w_ref[...])


def _layer_spec(layer, shape):
    return pl.BlockSpec((None,) + tuple(shape), lambda *_: (layer,) + (0,) * len(shape))


def _inproj(x2d, g, w, layer, tm):
    n = x2d.shape[0]
    return pl.pallas_call(
        _inproj_kernel,
        out_shape=jax.ShapeDtypeStruct((n, PROJ_COLS), F32),
        grid=(n // tm,),
        in_specs=[
            pl.BlockSpec((tm, D_MODEL), lambda i: (i, 0)),
            _layer_spec(layer, (1, D_MODEL)),
            _layer_spec(layer, (D_MODEL, PROJ_COLS)),
        ],
        out_specs=pl.BlockSpec((tm, PROJ_COLS), lambda i: (i, 0)),
        compiler_params=pltpu.CompilerParams(
            dimension_semantics=("arbitrary",), vmem_limit_bytes=VMEM_LIMIT),
        name="inproj",
    )(x2d, g, w)


def _memkv_kernel(m_ref, g_ref, wk_ref, wv_ref, k_ref, v_ref):
    h = _rms_norm(m_ref[...], g_ref[...]).astype(BF16)
    k_ref[...] = _dot(h, wk_ref[...]).astype(BF16)
    v_ref[...] = _dot(h, wv_ref[...]).astype(BF16)


def _memkv(mem2d, g, wk, wv, layer, tm):
    n = mem2d.shape[0]
    full = _layer_spec(layer, (D_MODEL, D_MODEL))
    tile = pl.BlockSpec((tm, D_MODEL), lambda i: (i, 0))
    return pl.pallas_call(
        _memkv_kernel,
        out_shape=(jax.ShapeDtypeStruct((n, D_MODEL), BF16),) * 2,
        grid=(n // tm,),
        in_specs=[tile, _layer_spec(layer, (1, D_MODEL)), full, full],
        out_specs=(tile, tile),
        compiler_params=pltpu.CompilerParams(
            dimension_semantics=("arbitrary",), vmem_limit_bytes=VMEM_LIMIT),
        name="memkv",
    )(mem2d, g, wk, wv)


def _decay_exponents(b, lf):
    w = b.shape[-1]
    groups = CHUNK // SUB
    b3 = b.reshape(groups, SUB, w)
    row = lax.broadcasted_iota(jnp.int32, (groups, SUB, w), 1)
    levels = []
    for m in (32, 16, 8):
        gm = m // SUB
        parts = []
        for blk in range(groups // (2 * gm)):
            g0 = blk * 2 * gm
            ref = b3[g0 + gm - 1:g0 + gm, SUB - 1:SUB, :]
            parts.append(ref - b3[g0:g0 + gm])
            parts.append(b3[g0 + gm:g0 + 2 * gm] - ref)
        levels.append(jnp.concatenate(parts, axis=0))
    d4 = b3 - b3[:, 3:4, :]
    levels.append(jnp.where(row >= 4, d4, -d4))
    d2 = b3 - jnp.where(row < 4, b3[:, 1:2, :], b3[:, 5:6, :])
    levels.append(jnp.where((row & 3) >= 2, d2, -d2))
    levels.append(jnp.where((row & 1) == 1, lf.reshape(groups, SUB, w), 0.0))
    e_kstate = b3[groups - 1:groups, SUB - 1:SUB, :] - b3
    return [e.reshape(CHUNK, w) for e in levels], e_kstate.reshape(CHUNK, w)


def _pair_masks():
    i = lax.broadcasted_iota(jnp.int32, (CHUNK, LANES), 0)
    j = lax.broadcasted_iota(jnp.int32, (CHUNK, LANES), 1) & (CHUNK - 1)
    x = jnp.bitwise_xor(i, j)
    below = i > j
    masks = []
    for lvl in range(N_LEVELS):
        top_bit = N_LEVELS - 1 - lvl
        masks.append(jnp.logical_and(below, jnp.right_shift(x, top_bit) == 1))
    return i == j, masks


def _block_rows(t0, t1):
    z = jnp.zeros_like(t0)
    return jnp.concatenate([jnp.concatenate([t0, z], axis=1), jnp.concatenate([z, t1], axis=1)], axis=0)


def _pair_scores(q16, levels, diag, masks, key_blocks):
    scores = jnp.where(diag, _dot_nt(q16, key_blocks(None)), 0.0)
    for lvl in range(N_LEVELS):
        ee = jnp.exp2(levels[lvl]).astype(BF16)
        p = _dot_nt(q16 * ee, key_blocks(ee))
        scores = jnp.where(masks[lvl], p, scores)
    return scores


def _lockstep(gens):
    live = list(gens)
    while live:
        live = [g for g in live if next(g, StopIteration) is not StopIteration]


def _mixer_kernel(proj_ref, tri_ref, lb_logit_ref, hnorm_ref, gup_ref, gbias_ref, gnorm_ref,
                  convw_ref, convb_ref, wqkv_ref, sbias_ref, skip_ref, mnorm_ref,
                  out_ref,
                  hst_ref, gst_ref, mc_ref, mn_ref, mm_ref, conv_ref, *, layer):
    @pl.when(pl.program_id(1) == 0)
    def _():
        hst_ref[...] = jnp.zeros_like(hst_ref)
        gst_ref[...] = jnp.zeros_like(gst_ref)
        mc_ref[...] = jnp.zeros_like(mc_ref)
        mn_ref[...] = jnp.zeros_like(mn_ref)
        mm_ref[...] = jnp.zeros_like(mm_ref)
        conv_ref[:, 0:SUB, :] = jnp.zeros((STEP_SEQS, SUB, N_HEADS * MLSTM_DH), F32)

    tri = tri_ref[...]
    diag, masks = _pair_masks()
    lane = lax.broadcasted_iota(jnp.int32, (CHUNK, LANES), 1)
    low_half = lane < GLA_DK
    is_f = jnp.logical_and(lane >= L_MF, lane < L_MF + N_HEADS)
    is_i = jnp.logical_and(lane >= L_MI, lane < L_MI + N_HEADS)
    lane_st = lax.broadcasted_iota(jnp.int32, (GLA_DV, LANES), 1)
    eye = (lax.broadcasted_iota(jnp.int32, (LANES, LANES), 0)
           == lax.broadcasted_iota(jnp.int32, (LANES, LANES), 1)).astype(BF16)
    ci = lax.broadcasted_iota(jnp.int32, (CHUNK, CHUNK), 0)
    cj = lax.broadcasted_iota(jnp.int32, (CHUNK, CHUNK), 1)
    causal = ci >= cj

    logits = lb_logit_ref[...]
    ex = jnp.exp(logits - jnp.max(logits, axis=0, keepdims=True))
    soft = ex / jnp.sum(ex, axis=0, keepdims=True)
    c0 = soft[0:1, :]
    cl = c0
    for r in range(1, layer + 1):
        cl = cl + soft[r:r + 1, :]
    lb = cl - c0
    one_m_lb = 1.0 - lb

    def chunk(sq, r0):
        rows = slice(r0, r0 + CHUNK)

        small = proj_ref[sq, rows, C_SMALL:C_SMALL + LANES]
        gla_pre = _dot(small.astype(BF16), gup_ref[...]) + gbias_ref[...]

        sig_f = _sigmoid(proj_ref[sq, rows, C_AF:C_AF + N_HEADS * HGRN_D])
        h_lf = jnp.maximum(jnp.log(lb + one_m_lb * sig_f) * LOG2E, LOG_ZERO)
        h_k = one_m_lb * (1.0 - sig_f)

        sb = small + sbias_ref[...]
        m_lf = jnp.where(is_f, _log_sigmoid(sb), 0.0)

        b_hm = _cumsum_rows(tri, jnp.concatenate([h_lf, m_lf], axis=1))
        h_b = b_hm[:, :N_HEADS * HGRN_D]
        m_b = b_hm[:, N_HEADS * HGRN_D:]
        yield

        m_u = proj_ref[sq, rows, C_MU:C_MU + N_HEADS * MLSTM_DH]
        conv_ref[sq, SUB:SUB + CHUNK, :] = m_u
        acc = convb_ref[...]
        for tap in range(MLSTM_CONV):
            off = SUB - (MLSTM_CONV - 1) + tap
            acc = acc + conv_ref[sq, off:off + CHUNK, :] * convw_ref[tap:tap + 1, :]
        conv_ref[sq, 0:SUB, :] = m_u[CHUNK - SUB:CHUNK, :]
        conv = _silu(acc)
        m_q, m_k, m_v = [], [], []
        for h in range(N_HEADS):
            sl = slice(h * MLSTM_DH, (h + 1) * MLSTM_DH)
            cb = conv[:, sl].astype(BF16)
            m_q.append(_dot(cb, wqkv_ref[0, h]))
            m_k.append(_dot(cb, wqkv_ref[1, h]) * (MLSTM_DH ** -0.5))
            m_v.append(_dot(m_u[:, sl].astype(BF16), wqkv_ref[2, h]).astype(BF16))
        yield

        z = jnp.where(is_i, sb, m_b)
        zh, zm, zl = _split3(z)
        z_t = _dot_nt(eye, zh) + _dot_nt(eye, zm) + _dot_nt(eye, zl)

        g_lf = _log_sigmoid(gla_pre) * (LOG2E / GLA_GATE_NORMALIZER)
        g_b = _cumsum_rows(tri, g_lf)
        yield

        h_q = _silu(proj_ref[sq, rows, C_AQ:C_AQ + N_HEADS * HGRN_D])
        h_scores, h_vb = [], []
        for pair in range(N_HEADS // 2):
            psl = slice(pair * 2 * HGRN_D, (pair + 1) * 2 * HGRN_D)
            q, k, b = h_q[:, psl], h_k[:, psl], h_b[:, psl]
            levels, e_k = _decay_exponents(b, h_lf[:, psl])
            q_inter = (q * jnp.exp2(b)).astype(BF16)
            k_state = (k * jnp.exp2(e_k)).astype(BF16)
            decay = jnp.exp2(b[CHUNK - 1:CHUNK, :])
            vb = proj_ref[sq, rows,
                          C_AI + pair * 2 * HGRN_D:C_AI + (pair + 1) * 2 * HGRN_D].astype(BF16)
            o_inter = []
            for hp in range(2):
                h = 2 * pair + hp
                hsl = slice(hp * HGRN_D, (hp + 1) * HGRN_D)
                st = hst_ref[sq, h]
                o_inter.append(_dot_nt(q_inter[:, hsl], st.astype(BF16)))
                hst_ref[sq, h] = decay[:, hsl] * st + _dot_tn(vb[:, hsl], k_state[:, hsl])
            k16 = k.astype(BF16)

            def h_keys(scale, k16=k16):
                t = k16 if scale is None else k16 * scale
                return _block_rows(t[:, :HGRN_D], t[:, HGRN_D:])

            h_scores.append((jnp.concatenate(o_inter, axis=1),
                             _pair_scores(q.astype(BF16), levels, diag, masks, h_keys)))
            h_vb.append(vb)
            yield

        m_parts = []
        for h in range(N_HEADS):
            q, k, vb = m_q[h], m_k[h], m_v[h]
            qb, kb = q.astype(BF16), k.astype(BF16)
            bc = z[:, L_MF + h:L_MF + h + 1]
            ic = z[:, L_MI + h:L_MI + h + 1]
            br = z_t[L_MF + h:L_MF + h + 1, :]
            ir = z_t[L_MI + h:L_MI + h + 1, :]
            m_prev = mm_ref[sq, h:h + 1, 0:1]
            c_prev = mc_ref[sq, h]
            n_prev = mn_ref[sq, h:h + 1, :]

            log_d = jnp.where(causal, bc - br + ir, -jnp.inf)
            m_inter = bc + m_prev
            m_i = jnp.maximum(m_inter, jnp.max(log_d, axis=-1, keepdims=True))
            w_inter = jnp.exp(m_inter - m_i)
            s = _dot_nt(qb, kb) * jnp.exp(log_d - m_i)
            num_inter = w_inter * _dot(qb, c_prev.astype(BF16))
            den = (w_inter * jnp.sum(q * n_prev, axis=-1, keepdims=True)
                   + jnp.sum(s, axis=-1, keepdims=True))
            scale = 1.0 / jnp.maximum(jnp.abs(den), jnp.exp(-m_i))

            b_last = bc[CHUNK - 1:CHUNK, :]
            log_w = b_last - bc + ic
            m_new = jnp.maximum(b_last + m_prev, jnp.max(log_w, axis=0, keepdims=True))
            w_prev = jnp.exp(b_last + m_prev - m_new)
            kw = jnp.exp(log_w - m_new) * k
            mc_ref[sq, h] = w_prev * c_prev + _dot_tn(kw.astype(BF16), vb)
            mn_ref[sq, h:h + 1, :] = w_prev * n_prev + jnp.sum(kw, axis=0, keepdims=True)
            mm_ref[sq, h:h + 1, :] = jnp.broadcast_to(m_new, (1, LANES))
            m_parts.append((num_inter, s.astype(BF16), scale))
        yield

        g_scores, g_vb = [], []
        for pair in range(N_HEADS // 2):
            psl = slice(pair * LANES, (pair + 1) * LANES)
            q = proj_ref[sq, rows, C_GQ + pair * LANES:C_GQ + (pair + 1) * LANES] * (GLA_DK ** -0.5)
            k = proj_ref[sq, rows, C_GK + pair * LANES:C_GK + (pair + 1) * LANES]
            b = g_b[:, psl]
            levels, e_k = _decay_exponents(b, g_lf[:, psl])
            q_inter = q * jnp.exp2(b)
            k_state = (k * jnp.exp2(e_k)).astype(BF16)
            decay = jnp.exp2(b[CHUNK - 1:CHUNK, :])
            vb = proj_ref[sq, rows,
                          C_GV + pair * 2 * GLA_DV:C_GV + (pair + 1) * 2 * GLA_DV].astype(BF16)
            st = gst_ref[sq, pair]
            stb = st.astype(BF16)
            o_inter = [_dot_nt(jnp.where(low_half, q_inter, 0.0).astype(BF16), stb),
                       _dot_nt(jnp.where(low_half, 0.0, q_inter).astype(BF16), stb)]
            gst_ref[sq, pair] = decay * st + jnp.where(lane_st < GLA_DK,
                                                       _dot_tn(vb[:, :GLA_DV], k_state),
                                                       _dot_tn(vb[:, GLA_DV:], k_state))
            k_lo = jnp.where(low_half, k, 0.0).astype(BF16)
            k_hi = jnp.where(low_half, 0.0, k).astype(BF16)

            def g_keys(scale, k_lo=k_lo, k_hi=k_hi):
                if scale is None:
                    return jnp.concatenate([k_lo, k_hi], axis=0)
                return jnp.concatenate([k_lo * scale, k_hi * scale], axis=0)

            g_scores.append((jnp.concatenate(o_inter, axis=1),
                             _pair_scores(q.astype(BF16), levels, diag, masks, g_keys)))
            g_vb.append(vb)
            yield

        for pair in range(N_HEADS // 2):
            o_inter, scores = h_scores[pair]
            vb = h_vb[pair]
            o = o_inter + _dot(scores.astype(BF16), _block_rows(vb[:, :HGRN_D], vb[:, HGRN_D:]))
            for hp in range(2):
                h = 2 * pair + hp
                oh = o[:, hp * HGRN_D:(hp + 1) * HGRN_D]
                y = oh * lax.rsqrt(jnp.mean(oh * oh, axis=-1, keepdims=True) + RMS_EPS) * hnorm_ref[...]
                y = y * _silu(proj_ref[sq, rows, C_AZ + h * HGRN_D:C_AZ + (h + 1) * HGRN_D])
                out_ref[sq, rows, h * 128:(h + 1) * 128] = y.astype(out_ref.dtype)
        yield

        for h in range(N_HEADS):
            sl = slice(h * MLSTM_DH, (h + 1) * MLSTM_DH)
            num_inter, sbf, scale = m_parts[h]
            hh = (num_inter + _dot(sbf, m_v[h])) * scale
            mu = jnp.mean(hh, axis=-1, keepdims=True)
            hc = hh - mu
            y = hc * lax.rsqrt(jnp.mean(hc * hc, axis=-1, keepdims=True) + RMS_EPS) * mnorm_ref[:, sl]
            y = (y + skip_ref[:, sl] * conv[:, sl]) * _silu(
                proj_ref[sq, rows, C_MZ + h * MLSTM_DH:C_MZ + (h + 1) * MLSTM_DH])
            out_ref[sq, rows, 1024 + h * 128:1024 + (h + 1) * 128] = y.astype(out_ref.dtype)
        yield

        for pair in range(N_HEADS // 2):
            o_inter, scores = g_scores[pair]
            vb = g_vb[pair]
            o = o_inter + _dot(scores.astype(BF16), _block_rows(vb[:, :GLA_DV], vb[:, GLA_DV:]))
            for hp in range(2):
                h = 2 * pair + hp
                oh = o[:, hp * GLA_DV:(hp + 1) * GLA_DV]
                y = oh * lax.rsqrt(jnp.mean(oh * oh, axis=-1, keepdims=True) + RMS_EPS) * gnorm_ref[...]
                y = y * _silu(proj_ref[sq, rows, C_GZ + h * GLA_DV:C_GZ + (h + 1) * GLA_DV])
                out_ref[sq, rows, 512 + h * 128:512 + (h + 1) * 128] = y.astype(out_ref.dtype)

    for c in range(proj_ref.shape[1] // CHUNK):
        _lockstep([chunk(sq, c * CHUNK) for sq in range(proj_ref.shape[0])])


def _mixers(proj, tri, lb_logits, hnorm, gup, gbias, gnorm, convw, convb, wqkv,
            sbias, skip, mnorm, layer):
    b, s, _ = proj.shape
    depth = lb_logits.shape[0]
    t = CHUNK * STEP_CHUNKS
    nsq = STEP_SEQS if b % STEP_SEQS == 0 else 1

    def whole(shape):
        return pl.BlockSpec(shape, lambda i, j: (0,) * len(shape))

    return pl.pallas_call(
        functools.partial(_mixer_kernel, layer=layer),
        out_shape=jax.ShapeDtypeStruct((b, s, MIX_WIDTH), BF16),
        grid=(b // nsq, s // t),
        in_specs=[
            pl.BlockSpec((nsq, t, PROJ_COLS), lambda i, j: (i, j, 0)),
            whole((CHUNK, CHUNK)),
            whole((depth, N_HEADS * HGRN_D)),
            _layer_spec(layer, (1, HGRN_D)),
            _layer_spec(layer, (LANES, N_HEADS * GLA_DK)),
            _layer_spec(layer, (1, N_HEADS * GLA_DK)),
            _layer_spec(layer, (1, GLA_DV)),
            _layer_spec(layer, (MLSTM_CONV, N_HEADS * MLSTM_DH)),
            _layer_spec(layer, (1, N_HEADS * MLSTM_DH)),
            pl.BlockSpec((3, None, N_HEADS, MLSTM_DH, MLSTM_DH), lambda i, j: (0, layer, 0, 0, 0)),
            _layer_spec(layer, (1, LANES)),
            _layer_spec(layer, (1, N_HEADS * MLSTM_DH)),
            _layer_spec(layer, (1, N_HEADS * MLSTM_DH)),
        ],
        out_specs=pl.BlockSpec((nsq, t, MIX_WIDTH), lambda i, j: (i, j, 0)),
        scratch_shapes=[
            pltpu.VMEM((nsq, N_HEADS, HGRN_D, HGRN_D), F32),
            pltpu.VMEM((nsq, N_HEADS // 2, GLA_DV, LANES), F32),
            pltpu.VMEM((nsq, N_HEADS, MLSTM_DH, MLSTM_DH), F32),
            pltpu.VMEM((nsq, SUB, MLSTM_DH), F32),
            pltpu.VMEM((nsq, SUB, LANES), F32),
            pltpu.VMEM((nsq, SUB + CHUNK, N_HEADS * MLSTM_DH), F32),
        ],
        compiler_params=pltpu.CompilerParams(
            dimension_semantics=("arbitrary", "arbitrary"), vmem_limit_bytes=VMEM_LIMIT),
        name="mixers",
    )(proj, tri, lb_logits, hnorm, gup, gbias, gnorm, convw, convb, wqkv, sbias, skip, mnorm)


def _outattn_kernel(mix_ref, x_ref, wout_ref, gx_ref, wq_ref, k_ref, v_ref, wo_ref, gf_ref,
                    o_ref, *, final_norm):
    tm = x_ref.shape[0]
    halves = [slice(a * (tm // 2), (a + 1) * (tm // 2)) for a in range(2)] if tm % 16 == 0 else [slice(0, tm)]
    x1 = [x_ref[r, :] + _dot(mix_ref[r, :], wout_ref[...]) for r in halves]
    q = [_dot(_rms_norm(t, gx_ref[...]).astype(BF16), wq_ref[...]).astype(BF16) for t in x1]
    items = [(a, hd) for a in range(len(halves)) for hd in range(N_HEADS)]

    def scores(item):
        a, hd = item
        sl = slice(hd * XA_HEAD_DIM, (hd + 1) * XA_HEAD_DIM)
        return _dot_nt(q[a][:, sl], k_ref[:, sl]) * (XA_HEAD_DIM ** -0.5)

    lookahead = 2
    pending = [scores(it) for it in items[:lookahead]]
    outs = [[] for _ in halves]
    for n, (a, hd) in enumerate(items):
        s = pending.pop(0)
        e = jnp.exp(s - jnp.max(s, axis=-1, keepdims=True))
        p = e / jnp.sum(e, axis=-1, keepdims=True)
        sl = slice(hd * XA_HEAD_DIM, (hd + 1) * XA_HEAD_DIM)
        outs[a].append(_dot(p.astype(BF16), v_ref[:, sl]).astype(BF16))
        if n + lookahead < len(items):
            pending.append(scores(items[n + lookahead]))
    for a, r in enumerate(halves):
        x2 = x1[a] + _dot(jnp.concatenate(outs[a], axis=-1), wo_ref[...])
        if final_norm:
            x2 = _rms_norm(x2, gf_ref[...])
        o_ref[r, :] = x2


def _outattn(mix, x, wout, gx, wq, kmem, vmem, wo, gf, layer, tm, final_norm):
    b, s, _ = x.shape
    m = kmem.shape[1]

    return pl.pallas_call(
        functools.partial(_outattn_kernel, final_norm=final_norm),
        out_shape=jax.ShapeDtypeStruct((b, s, D_MODEL), F32),
        grid=(b, s // tm),
        in_specs=[
            pl.BlockSpec((None, tm, MIX_WIDTH), lambda i, j: (i, j, 0)),
            pl.BlockSpec((None, tm, D_MODEL), lambda i, j: (i, j, 0)),
            _layer_spec(layer, (MIX_WIDTH, D_MODEL)),
            _layer_spec(layer, (1, D_MODEL)),
            _layer_spec(layer, (D_MODEL, D_MODEL)),
            pl.BlockSpec((None, m, D_MODEL), lambda i, j: (i, 0, 0)),
            pl.BlockSpec((None, m, D_MODEL), lambda i, j: (i, 0, 0)),
            _layer_spec(layer, (D_MODEL, D_MODEL)),
            pl.BlockSpec((1, D_MODEL), lambda i, j: (0, 0)),
        ],
        out_specs=pl.BlockSpec((None, tm, D_MODEL), lambda i, j: (i, j, 0)),
        compiler_params=pltpu.CompilerParams(
            dimension_semantics=("arbitrary", "arbitrary"), vmem_limit_bytes=VMEM_LIMIT),
        name="outattn",
    )(mix, x, wout, gx, wq, kmem, vmem, wo, gf)


def _pack_w_in(w):
    ga0 = 2048 + 2 * N_HEADS * GLA_DK + N_HEADS * GLA_DV
    ga1 = ga0 + GLA_RANK
    gate0 = ga1 + N_HEADS * GLA_DV + 2 * N_HEADS * MLSTM_DH
    pad = jnp.zeros(w.shape[:-1] + (LANES - GLA_RANK - 2 * N_HEADS,), w.dtype)
    return jnp.concatenate([w[..., :ga0], w[..., ga1:gate0], w[..., ga0:ga1], w[..., gate0:], pad],
                           axis=-1).astype(BF16)


def _block_diag_dense(w):
    lead = w.shape[:-3]
    blk = w.shape[-1]
    rows = w.reshape(lead + (N_HEADS, MLSTM_DH, blk))
    tiled = jnp.tile(rows, (1,) * (len(lead) + 2) + (MLSTM_DH // blk,))
    r = np.arange(MLSTM_DH)
    on_diag = jnp.asarray((r[:, None] // blk) == (r[None, :] // blk))
    return jnp.where(on_diag, tiled, 0.0).astype(BF16)


def kernel(x, mem, norm_mix, w_in, hgrn_lb_logits, hgrn_norm, gla_gate_up, gla_gate_bias, gla_norm,
           mlstm_conv_w, mlstm_conv_b, mlstm_wq, mlstm_wk, mlstm_wv, mlstm_igate_bias,
           mlstm_fgate_bias, mlstm_skip, mlstm_norm, w_out, norm_xattn, norm_mem, xa_wq, xa_wk,
           xa_wv, xa_wo, norm_final):
    b, s, d = x.shape
    m = mem.shape[1]
    depth = w_in.shape[0]
    tri = jnp.asarray(np.tril(np.ones((CHUNK, CHUNK), np.float32)), dtype=BF16)
    mem2d = mem.reshape(b * m, d)

    def row(p):
        return p[:, None, :]

    w_in_p = _pack_w_in(w_in)
    gup = jnp.concatenate(
        [gla_gate_up, jnp.zeros((depth, LANES - GLA_RANK, N_HEADS * GLA_DK), F32)], axis=1).astype(BF16)
    sbias = row(jnp.concatenate(
        [jnp.zeros((depth, GLA_RANK), F32), mlstm_igate_bias, mlstm_fgate_bias,
         jnp.zeros((depth, LANES - GLA_RANK - 2 * N_HEADS), F32)], axis=1))
    wqkv = _block_diag_dense(jnp.stack([mlstm_wq, mlstm_wk, mlstm_wv]))
    w_out_b, xa_wq_b, xa_wk_b, xa_wv_b, xa_wo_b = (
        t.astype(BF16) for t in (w_out, xa_wq, xa_wk, xa_wv, xa_wo))

    for l in range(depth):
        proj = _inproj(x.reshape(b * s, d), row(norm_mix), w_in_p, layer=l, tm=min(512, b * s))
        mix = _mixers(
            proj.reshape(b, s, PROJ_COLS), tri, hgrn_lb_logits, row(hgrn_norm), gup,
            row(gla_gate_bias), row(gla_norm), mlstm_conv_w, row(mlstm_conv_b), wqkv, sbias,
            row(mlstm_skip), row(mlstm_norm), layer=l)
        kmem, vmem = _memkv(mem2d, row(norm_mem), xa_wk_b, xa_wv_b, layer=l, tm=256)
        x = _outattn(mix, x, w_out_b, row(norm_xattn), xa_wq_b, kmem.reshape(b, m, d),
                     vmem.reshape(b, m, d), xa_wo_b, norm_final[None, :], layer=l,
                     tm=min(512, s), final_norm=(l == depth - 1))
    return x
```

```python
import functools
import math

import numpy as np
import jax
import jax.numpy as jnp
from jax import lax
from jax.experimental import pallas as pl
from jax.experimental.pallas import tpu as pltpu

F32 = jnp.float32
BF16 = jnp.bfloat16

RMS_EPS = 1e-6
D_MODEL = 1024
CHUNK = 64
STEP_CHUNKS = 4
STEP_SEQS = 2
SUB = 8
LANES = 128
N_HEADS = 4
HGRN_D = 128
GLA_DK = 64
GLA_DV = 128
GLA_RANK = 16
GLA_GATE_NORMALIZER = 16.0
MLSTM_DH = 128
MLSTM_CONV = 4
XA_HEAD_DIM = D_MODEL // N_HEADS
MIX_WIDTH = 3 * N_HEADS * 128

C_AQ, C_AF, C_AI, C_AZ = 0, 512, 1024, 1536
C_GQ, C_GK, C_GV, C_GZ = 2048, 2304, 2560, 3072
C_MU, C_MZ = 3584, 4096
C_SMALL = 4608
PROJ_COLS = C_SMALL + LANES
L_MI = GLA_RANK
L_MF = GLA_RANK + N_HEADS

N_LEVELS = 6
LOG_ZERO = -1e30
LOG2E = math.log2(math.e)
VMEM_LIMIT = 48 * 1024 * 1024


def _exp_neg(x):
    return jnp.exp2(x * (-LOG2E))


def _sigmoid(x):
    return 1.0 / (1.0 + _exp_neg(x))


def _silu(x):
    return x * _sigmoid(x)


def _log_sigmoid(x):
    return jnp.minimum(x, 0.0) - jnp.log(1.0 + _exp_neg(jnp.abs(x)))


def _dot(a, b):
    return jnp.dot(a, b, preferred_element_type=F32)


def _dot_nt(a, b):
    return lax.dot_general(a, b, (((1,), (1,)), ((), ())), preferred_element_type=F32)


def _dot_tn(a, b):
    return lax.dot_general(a, b, (((0,), (0,)), ((), ())), preferred_element_type=F32)


def _rms_norm(x, g):
    return x * lax.rsqrt(jnp.mean(x * x, axis=-1, keepdims=True) + RMS_EPS) * g


def _split3(x):
    hi = x.astype(BF16)
    r1 = x - hi.astype(F32)
    mid = r1.astype(BF16)
    lo = (r1 - mid.astype(F32)).astype(BF16)
    return hi, mid, lo


def _cumsum_rows(tri_bf16, x):
    hi = x.astype(BF16)
    lo = (x - hi.astype(F32)).astype(BF16)
    return _dot(tri_bf16, hi) + _dot(tri_bf16, lo)


def _layer_spec(layer, shape):
    return pl.BlockSpec((None,) + tuple(shape), lambda *_: (layer,) + (0,) * len(shape))


def _inproj_kernel(x_ref, g_ref, w_ref, o_ref):
    o_ref[...] = _dot(_rms_norm(x_ref[...], g_ref[...]).astype(BF16), w_ref[...])


def _inproj(x2d, g, w, layer, tm):
    n = x2d.shape[0]
    return pl.pallas_call(
        _inproj_kernel,
        out_shape=jax.ShapeDtypeStruct((n, PROJ_COLS), F32),
        grid=(n // tm,),
        in_specs=[
            pl.BlockSpec((tm, D_MODEL), lambda i: (i, 0)),
            _layer_spec(layer, (1, D_MODEL)),
            _layer_spec(layer, (D_MODEL, PROJ_COLS)),
        ],
        out_specs=pl.BlockSpec((tm, PROJ_COLS), lambda i: (i, 0)),
        compiler_params=pltpu.CompilerParams(
            dimension_semantics=("arbitrary",), vmem_limit_bytes=VMEM_LIMIT),
        name="inproj",
    )(x2d, g, w)


def _memkv_kernel(m_ref, g_ref, wk_ref, wv_ref, k_ref, v_ref):
    h = _rms_norm(m_ref[...], g_ref[...]).astype(BF16)
    k_ref[...] = _dot(h, wk_ref[...]).astype(BF16)
    v_ref[...] = _dot(h, wv_ref[...]).astype(BF16)


def _memkv(mem2d, g, wk, wv, layer, tm):
    n = mem2d.shape[0]
    full = _layer_spec(layer, (D_MODEL, D_MODEL))
    tile = pl.BlockSpec((tm, D_MODEL), lambda i: (i, 0))
    return pl.pallas_call(
        _memkv_kernel,
        out_shape=(jax.ShapeDtypeStruct((n, D_MODEL), BF16),) * 2,
        grid=(n // tm,),
        in_specs=[tile, _layer_spec(layer, (1, D_MODEL)), full, full],
        out_specs=(tile, tile),
        compiler_params=pltpu.CompilerParams(
            dimension_semantics=("arbitrary",), vmem_limit_bytes=VMEM_LIMIT),
        name="memkv",
    )(mem2d, g, wk, wv)


def _row_signs(w):
    row = lax.broadcasted_iota(jnp.int32, (CHUNK // SUB, SUB, w), 1)
    sgn4 = jnp.where(row >= 4, 1.0, -1.0)
    sgn2 = jnp.where((row & 3) >= 2, 1.0, -1.0)
    odd = jnp.where((row & 1) == 1, 1.0, 0.0)
    return row < 4, sgn4, sgn2, odd


def _decay_exponents(b, lf, signs):
    w = b.shape[-1]
    groups = CHUNK // SUB
    first4, sgn4, sgn2, odd = (t[:, :, :w] for t in signs)
    b3 = b.reshape(groups, SUB, w)
    levels = []
    for m in (32, 16, 8):
        gm = m // SUB
        parts = []
        for blk in range(groups // (2 * gm)):
            g0 = blk * 2 * gm
            ref = b3[g0 + gm - 1:g0 + gm, SUB - 1:SUB, :]
            parts.append(ref - b3[g0:g0 + gm])
            parts.append(b3[g0 + gm:g0 + 2 * gm] - ref)
        levels.append(jnp.concatenate(parts, axis=0))
    levels.append((b3 - b3[:, 3:4, :]) * sgn4)
    levels.append((b3 - jnp.where(first4, b3[:, 1:2, :], b3[:, 5:6, :])) * sgn2)
    levels.append(lf.reshape(groups, SUB, w) * odd)
    e_kstate = b3[groups - 1:groups, SUB - 1:SUB, :] - b3
    return [e.reshape(CHUNK, w) for e in levels], e_kstate.reshape(CHUNK, w)


def _pair_masks():
    i = lax.broadcasted_iota(jnp.int32, (CHUNK, LANES), 0)
    j = lax.broadcasted_iota(jnp.int32, (CHUNK, LANES), 1) & (CHUNK - 1)
    x = jnp.bitwise_xor(i, j)
    below = i > j
    masks = [jnp.where(i == j, 1.0, 0.0)]
    for lvl in range(N_LEVELS):
        top_bit = N_LEVELS - 1 - lvl
        masks.append(jnp.where(jnp.logical_and(below, jnp.right_shift(x, top_bit) == 1), 1.0, 0.0))
    return masks


def _block_rows(t0, t1):
    z = jnp.zeros_like(t0)
    return jnp.concatenate([jnp.concatenate([t0, z], axis=1), jnp.concatenate([z, t1], axis=1)], axis=0)


def _pair_scores(q16, levels, masks, key_blocks):
    scores = _dot_nt(q16, key_blocks(None)) * masks[0]
    for lvl in range(N_LEVELS):
        ee = jnp.exp2(levels[lvl]).astype(BF16)
        scores = scores + _dot_nt(q16 * ee, key_blocks(ee)) * masks[1 + lvl]
    return scores


def _lockstep(gens):
    live = list(gens)
    while live:
        live = [g for g in live if next(g, StopIteration) is not StopIteration]


def _mixer_kernel(proj_ref, tri_ref, lb_logit_ref, hnorm_ref, gup_ref, gbias_ref, gnorm_ref,
                  convw_ref, convb_ref, wqkv_ref, sbias_ref, skip_ref, mnorm_ref,
                  out_ref,
                  hst_ref, gst_ref, mc_ref, mn_ref, mm_ref, conv_ref, *, layer):
    nsq = proj_ref.shape[0]

    @pl.when(pl.program_id(1) == 0)
    def _():
        hst_ref[...] = jnp.zeros_like(hst_ref)
        gst_ref[...] = jnp.zeros_like(gst_ref)
        mc_ref[...] = jnp.zeros_like(mc_ref)
        mn_ref[...] = jnp.zeros_like(mn_ref)
        mm_ref[...] = jnp.zeros_like(mm_ref)
        conv_ref[:, 0:SUB, :] = jnp.zeros((nsq, SUB, N_HEADS * MLSTM_DH), F32)

    tri = tri_ref[...]
    masks = _pair_masks()
    signs = _row_signs(2 * HGRN_D)
    lane = lax.broadcasted_iota(jnp.int32, (CHUNK, LANES), 1)
    low_half = lane < GLA_DK
    is_f = jnp.logical_and(lane >= L_MF, lane < L_MF + N_HEADS)
    is_i = jnp.logical_and(lane >= L_MI, lane < L_MI + N_HEADS)
    lane_st = lax.broadcasted_iota(jnp.int32, (GLA_DV, LANES), 1)
    eye = (lax.broadcasted_iota(jnp.int32, (LANES, LANES), 0)
           == lax.broadcasted_iota(jnp.int32, (LANES, LANES), 1)).astype(BF16)
    ci = lax.broadcasted_iota(jnp.int32, (CHUNK, CHUNK), 0)
    cj = lax.broadcasted_iota(jnp.int32, (CHUNK, CHUNK), 1)
    causal = ci >= cj

    logits = lb_logit_ref[...]
    ex = jnp.exp(logits - jnp.max(logits, axis=0, keepdims=True))
    soft = ex / jnp.sum(ex, axis=0, keepdims=True)
    c0 = soft[0:1, :]
    cl = c0
    for r in range(1, layer + 1):
        cl = cl + soft[r:r + 1, :]
    lb = cl - c0
    one_m_lb = 1.0 - lb

    def chunk(sq, r0):
        rows = slice(r0, r0 + CHUNK)

        small = proj_ref[sq, rows, C_SMALL:C_SMALL + LANES]
        gla_pre = _dot(small.astype(BF16), gup_ref[...]) + gbias_ref[...]

        sig_f = _sigmoid(proj_ref[sq, rows, C_AF:C_AF + N_HEADS * HGRN_D])
        h_lf = jnp.maximum(jnp.log(lb + one_m_lb * sig_f) * LOG2E, LOG_ZERO)
        h_k = one_m_lb * (1.0 - sig_f)

        sb = small + sbias_ref[...]
        m_lf = jnp.where(is_f, _log_sigmoid(sb), 0.0)

        b_hm = _cumsum_rows(tri, jnp.concatenate([h_lf, m_lf], axis=1))
        h_b = b_hm[:, :N_HEADS * HGRN_D]
        m_b = b_hm[:, N_HEADS * HGRN_D:]
        yield

        m_u = proj_ref[sq, rows, C_MU:C_MU + N_HEADS * MLSTM_DH]
        conv_ref[sq, SUB:SUB + CHUNK, :] = m_u
        acc = convb_ref[...]
        for tap in range(MLSTM_CONV):
            off = SUB - (MLSTM_CONV - 1) + tap
            acc = acc + conv_ref[sq, off:off + CHUNK, :] * convw_ref[tap:tap + 1, :]
        conv_ref[sq, 0:SUB, :] = m_u[CHUNK - SUB:CHUNK, :]
        conv = _silu(acc)
        m_q, m_k, m_v = [], [], []
        for h in range(N_HEADS):
            sl = slice(h * MLSTM_DH, (h + 1) * MLSTM_DH)
            cb = conv[:, sl].astype(BF16)
            m_q.append(_dot(cb, wqkv_ref[0, h]))
            m_k.append(_dot(cb, wqkv_ref[1, h]) * (MLSTM_DH ** -0.5))
            m_v.append(_dot(m_u[:, sl].astype(BF16), wqkv_ref[2, h]).astype(BF16))
        yield

        z = jnp.where(is_i, sb, m_b)
        zh, zm, zl = _split3(z)
        z_t = _dot_nt(eye, zh) + _dot_nt(eye, zm) + _dot_nt(eye, zl)

        g_lf = _log_sigmoid(gla_pre) * (LOG2E / GLA_GATE_NORMALIZER)
        g_b = _cumsum_rows(tri, g_lf)
        yield

        h_q = _silu(proj_ref[sq, rows, C_AQ:C_AQ + N_HEADS * HGRN_D])
        h_scores, h_vb = [], []
        for pair in range(N_HEADS // 2):
            psl = slice(pair * 2 * HGRN_D, (pair + 1) * 2 * HGRN_D)
            q, k, b = h_q[:, psl], h_k[:, psl], h_b[:, psl]
            levels, e_k = _decay_exponents(b, h_lf[:, psl], signs)
            q_inter = (q * jnp.exp2(b)).astype(BF16)
            k_state = (k * jnp.exp2(e_k)).astype(BF16)
            decay = jnp.exp2(b[CHUNK - 1:CHUNK, :])
            vb = proj_ref[sq, rows,
                          C_AI + pair * 2 * HGRN_D:C_AI + (pair + 1) * 2 * HGRN_D].astype(BF16)
            o_inter = []
            for hp in range(2):
                h = 2 * pair + hp
                hsl = slice(hp * HGRN_D, (hp + 1) * HGRN_D)
                st = hst_ref[sq, h]
                o_inter.append(_dot_nt(q_inter[:, hsl], st.astype(BF16)))
                hst_ref[sq, h] = decay[:, hsl] * st + _dot_tn(vb[:, hsl], k_state[:, hsl])
            k16 = k.astype(BF16)

            def h_keys(scale, k16=k16):
                t = k16 if scale is None else k16 * scale
                return _block_rows(t[:, :HGRN_D], t[:, HGRN_D:])

            h_scores.append((jnp.concatenate(o_inter, axis=1),
                             _pair_scores(q.astype(BF16), levels, masks, h_keys)))
            h_vb.append(vb)
            yield

        m_parts = []
        for h in range(N_HEADS):
            q, k, vb = m_q[h], m_k[h], m_v[h]
            qb, kb = q.astype(BF16), k.astype(BF16)
            bc = z[:, L_MF + h:L_MF + h + 1]
            ic = z[:, L_MI + h:L_MI + h + 1]
            br = z_t[L_MF + h:L_MF + h + 1, :]
            ir = z_t[L_MI + h:L_MI + h + 1, :]
            m_prev = mm_ref[sq, h:h + 1, 0:1]
            c_prev = mc_ref[sq, h]
            n_prev = mn_ref[sq, h:h + 1, :]

            log_d = jnp.where(causal, bc - br + ir, -jnp.inf)
            m_inter = bc + m_prev
            m_i = jnp.maximum(m_inter, jnp.max(log_d, axis=-1, keepdims=True))
            w_inter = jnp.exp(m_inter - m_i)
            s = _dot_nt(qb, kb) * jnp.exp(log_d - m_i)
            num_inter = w_inter * _dot(qb, c_prev.astype(BF16))
            den = (w_inter * jnp.sum(q * n_prev, axis=-1, keepdims=True)
                   + jnp.sum(s, axis=-1, keepdims=True))
            scale = 1.0 / jnp.maximum(jnp.abs(den), jnp.exp(-m_i))

            b_last = bc[CHUNK - 1:CHUNK, :]
            log_w = b_last - bc + ic
            m_new = jnp.maximum(b_last + m_prev, jnp.max(log_w, axis=0, keepdims=True))
            w_prev = jnp.exp(b_last + m_prev - m_new)
            kw = jnp.exp(log_w - m_new) * k
            mc_ref[sq, h] = w_prev * c_prev + _dot_tn(kw.astype(BF16), vb)
            mn_ref[sq, h:h + 1, :] = w_prev * n_prev + jnp.sum(kw, axis=0, keepdims=True)
            mm_ref[sq, h:h + 1, :] = jnp.broadcast_to(m_new, (1, LANES))
            m_parts.append((num_inter, s.astype(BF16), scale))
        yield

        g_scores, g_vb = [], []
        for pair in range(N_HEADS // 2):
            psl = slice(pair * LANES, (pair + 1) * LANES)
            q = proj_ref[sq, rows, C_GQ + pair * LANES:C_GQ + (pair + 1) * LANES] * (GLA_DK ** -0.5)
            k = proj_ref[sq, rows, C_GK + pair * LANES:C_GK + (pair + 1) * LANES]
            b = g_b[:, psl]
            levels, e_k = _decay_exponents(b, g_lf[:, psl], signs)
            q_inter = q * jnp.exp2(b)
            k_state = (k * jnp.exp2(e_k)).astype(BF16)
            decay = jnp.exp2(b[CHUNK - 1:CHUNK, :])
            vb = proj_ref[sq, rows,
                          C_GV + pair * 2 * GLA_DV:C_GV + (pair + 1) * 2 * GLA_DV].astype(BF16)
            st = gst_ref[sq, pair]
            stb = st.astype(BF16)
            o_inter = [_dot_nt(jnp.where(low_half, q_inter, 0.0).astype(BF16), stb),
                       _dot_nt(jnp.where(low_half, 0.0, q_inter).astype(BF16), stb)]
            gst_ref[sq, pair] = decay * st + jnp.where(lane_st < GLA_DK,
                                                       _dot_tn(vb[:, :GLA_DV], k_state),
                                                       _dot_tn(vb[:, GLA_DV:], k_state))
            k_lo = jnp.where(low_half, k, 0.0).astype(BF16)
            k_hi = jnp.where(low_half, 0.0, k).astype(BF16)

            def g_keys(scale, k_lo=k_lo, k_hi=k_hi):
                if scale is None:
                    return jnp.concatenate([k_lo, k_hi], axis=0)
                return jnp.concatenate([k_lo * scale, k_hi * scale], axis=0)

            g_scores.append((jnp.concatenate(o_inter, axis=1),
                             _pair_scores(q.astype(BF16), levels, masks, g_keys)))
            g_vb.append(vb)
            yield

        for pair in range(N_HEADS // 2):
            o_inter, scores = h_scores[pair]
            vb = h_vb[pair]
            o = o_inter + _dot(scores.astype(BF16), _block_rows(vb[:, :HGRN_D], vb[:, HGRN_D:]))
            for hp in range(2):
                h = 2 * pair + hp
                oh = o[:, hp * HGRN_D:(hp + 1) * HGRN_D]
                y = oh * lax.rsqrt(jnp.mean(oh * oh, axis=-1, keepdims=True) + RMS_EPS) * hnorm_ref[...]
                y = y * _silu(proj_ref[sq, rows, C_AZ + h * HGRN_D:C_AZ + (h + 1) * HGRN_D])
                out_ref[sq, rows, h * 128:(h + 1) * 128] = y.astype(out_ref.dtype)
        yield

        for h in range(N_HEADS):
            sl = slice(h * MLSTM_DH, (h + 1) * MLSTM_DH)
            num_inter, sbf, scale = m_parts[h]
            hh = (num_inter + _dot(sbf, m_v[h])) * scale
            mu = jnp.mean(hh, axis=-1, keepdims=True)
            hc = hh - mu
            y = hc * lax.rsqrt(jnp.mean(hc * hc, axis=-1, keepdims=True) + RMS_EPS) * mnorm_ref[:, sl]
            y = (y + skip_ref[:, sl] * conv[:, sl]) * _silu(
                proj_ref[sq, rows, C_MZ + h * MLSTM_DH:C_MZ + (h + 1) * MLSTM_DH])
            out_ref[sq, rows, 1024 + h * 128:1024 + (h + 1) * 128] = y.astype(out_ref.dtype)
        yield

        for pair in range(N_HEADS // 2):
            o_inter, scores = g_scores[pair]
            vb = g_vb[pair]
            o = o_inter + _dot(scores.astype(BF16), _block_rows(vb[:, :GLA_DV], vb[:, GLA_DV:]))
            for hp in range(2):
                h = 2 * pair + hp
                oh = o[:, hp * GLA_DV:(hp + 1) * GLA_DV]
                y = oh * lax.rsqrt(jnp.mean(oh * oh, axis=-1, keepdims=True) + RMS_EPS) * gnorm_ref[...]
                y = y * _silu(proj_ref[sq, rows, C_GZ + h * GLA_DV:C_GZ + (h + 1) * GLA_DV])
                out_ref[sq, rows, 512 + h * 128:512 + (h + 1) * 128] = y.astype(out_ref.dtype)

    for c in range(proj_ref.shape[1] // CHUNK):
        _lockstep([chunk(sq, c * CHUNK) for sq in range(nsq)])


def _mixers(proj, tri, lb_logits, hnorm, gup, gbias, gnorm, convw, convb, wqkv,
            sbias, skip, mnorm, layer):
    b, s, _ = proj.shape
    depth = lb_logits.shape[0]
    t = min(CHUNK * STEP_CHUNKS, s)
    nsq = STEP_SEQS if b % STEP_SEQS == 0 else 1

    def whole(shape):
        return pl.BlockSpec(shape, lambda i, j: (0,) * len(shape))

    return pl.pallas_call(
        functools.partial(_mixer_kernel, layer=layer),
        out_shape=jax.ShapeDtypeStruct((b, s, MIX_WIDTH), BF16),
        grid=(b // nsq, s // t),
        in_specs=[
            pl.BlockSpec((nsq, t, PROJ_COLS), lambda i, j: (i, j, 0)),
            whole((CHUNK, CHUNK)),
            whole((depth, N_HEADS * HGRN_D)),
            _layer_spec(layer, (1, HGRN_D)),
            _layer_spec(layer, (LANES, N_HEADS * GLA_DK)),
            _layer_spec(layer, (1, N_HEADS * GLA_DK)),
            _layer_spec(layer, (1, GLA_DV)),
            _layer_spec(layer, (MLSTM_CONV, N_HEADS * MLSTM_DH)),
            _layer_spec(layer, (1, N_HEADS * MLSTM_DH)),
            pl.BlockSpec((3, None, N_HEADS, MLSTM_DH, MLSTM_DH), lambda i, j: (0, layer, 0, 0, 0)),
            _layer_spec(layer, (1, LANES)),
            _layer_spec(layer, (1, N_HEADS * MLSTM_DH)),
            _layer_spec(layer, (1, N_HEADS * MLSTM_DH)),
        ],
        out_specs=pl.BlockSpec((nsq, t, MIX_WIDTH), lambda i, j: (i, j, 0)),
        scratch_shapes=[
            pltpu.VMEM((nsq, N_HEADS, HGRN_D, HGRN_D), F32),
            pltpu.VMEM((nsq, N_HEADS // 2, GLA_DV, LANES), F32),
            pltpu.VMEM((nsq, N_HEADS, MLSTM_DH, MLSTM_DH), F32),
            pltpu.VMEM((nsq, SUB, MLSTM_DH), F32),
            pltpu.VMEM((nsq, SUB, LANES), F32),
            pltpu.VMEM((nsq, SUB + CHUNK, N_HEADS * MLSTM_DH), F32),
        ],
        compiler_params=pltpu.CompilerParams(
            dimension_semantics=("arbitrary", "arbitrary"), vmem_limit_bytes=VMEM_LIMIT),
        name="mixers",
    )(proj, tri, lb_logits, hnorm, gup, gbias, gnorm, convw, convb, wqkv, sbias, skip, mnorm)


def _outattn_kernel(mix_ref, x_ref, wout_ref, gx_ref, wq_ref, k_ref, v_ref, wo_ref, gf_ref,
                    o_ref, *, final_norm):
    tm = x_ref.shape[0]
    halves = [slice(a * (tm // 2), (a + 1) * (tm // 2)) for a in range(2)] if tm % 16 == 0 else [slice(0, tm)]
    x1 = [x_ref[r, :] + _dot(mix_ref[r, :], wout_ref[...]) for r in halves]
    q = [_dot(_rms_norm(t, gx_ref[...]).astype(BF16), wq_ref[...]).astype(BF16) for t in x1]
    items = [(a, hd) for a in range(len(halves)) for hd in range(N_HEADS)]

    def scores(item):
        a, hd = item
        sl = slice(hd * XA_HEAD_DIM, (hd + 1) * XA_HEAD_DIM)
        return _dot_nt(q[a][:, sl], k_ref[:, sl]) * (XA_HEAD_DIM ** -0.5)

    lookahead = 2
    pending = [scores(it) for it in items[:lookahead]]
    outs = [[] for _ in halves]
    for n, (a, hd) in enumerate(items):
        s = pending.pop(0)
        e = jnp.exp(s - jnp.max(s, axis=-1, keepdims=True))
        p = e / jnp.sum(e, axis=-1, keepdims=True)
        sl = slice(hd * XA_HEAD_DIM, (hd + 1) * XA_HEAD_DIM)
        outs[a].append(_dot(p.astype(BF16), v_ref[:, sl]).astype(BF16))
        if n + lookahead < len(items):
            pending.append(scores(items[n + lookahead]))
    for a, r in enumerate(halves):
        x2 = x1[a] + _dot(jnp.concatenate(outs[a], axis=-1), wo_ref[...])
        if final_norm:
            x2 = _rms_norm(x2, gf_ref[...])
        o_ref[r, :] = x2


def _outattn(mix, x, wout, gx, wq, kmem, vmem, wo, gf, layer, tm, final_norm):
    b, s, _ = x.shape
    m = kmem.shape[1]

    return pl.pallas_call(
        functools.partial(_outattn_kernel, final_norm=final_norm),
        out_shape=jax.ShapeDtypeStruct((b, s, D_MODEL), F32),
        grid=(b, s // tm),
        in_specs=[
            pl.BlockSpec((None, tm, MIX_WIDTH), lambda i, j: (i, j, 0)),
            pl.BlockSpec((None, tm, D_MODEL), lambda i, j: (i, j, 0)),
            _layer_spec(layer, (MIX_WIDTH, D_MODEL)),
            _layer_spec(layer, (1, D_MODEL)),
            _layer_spec(layer, (D_MODEL, D_MODEL)),
            pl.BlockSpec((None, m, D_MODEL), lambda i, j: (i, 0, 0)),
            pl.BlockSpec((None, m, D_MODEL), lambda i, j: (i, 0, 0)),
            _layer_spec(layer, (D_MODEL, D_MODEL)),
            pl.BlockSpec((1, D_MODEL), lambda i, j: (0, 0)),
        ],
        out_specs=pl.BlockSpec((None, tm, D_MODEL), lambda i, j: (i, j, 0)),
        compiler_params=pltpu.CompilerParams(
            dimension_semantics=("arbitrary", "arbitrary"), vmem_limit_bytes=VMEM_LIMIT),
        name="outattn",
    )(mix, x, wout, gx, wq, kmem, vmem, wo, gf)


def _pack_w_in(w):
    ga0 = 2048 + 2 * N_HEADS * GLA_DK + N_HEADS * GLA_DV
    ga1 = ga0 + GLA_RANK
    gate0 = ga1 + N_HEADS * GLA_DV + 2 * N_HEADS * MLSTM_DH
    pad = jnp.zeros(w.shape[:-1] + (LANES - GLA_RANK - 2 * N_HEADS,), w.dtype)
    return jnp.concatenate([w[..., :ga0], w[..., ga1:gate0], w[..., ga0:ga1], w[..., gate0:], pad],
                           axis=-1).astype(BF16)


def _block_diag_dense(w):
    lead = w.shape[:-3]
    blk = w.shape[-1]
    rows = w.reshape(lead + (N_HEADS, MLSTM_DH, blk))
    tiled = jnp.tile(rows, (1,) * (len(lead) + 2) + (MLSTM_DH // blk,))
    r = np.arange(MLSTM_DH)
    on_diag = jnp.asarray((r[:, None] // blk) == (r[None, :] // blk))
    return jnp.where(on_diag, tiled, 0.0).astype(BF16)


def kernel(x, mem, norm_mix, w_in, hgrn_lb_logits, hgrn_norm, gla_gate_up, gla_gate_bias, gla_norm,
           mlstm_conv_w, mlstm_conv_b, mlstm_wq, mlstm_wk, mlstm_wv, mlstm_igate_bias,
           mlstm_fgate_bias, mlstm_skip, mlstm_norm, w_out, norm_xattn, norm_mem, xa_wq, xa_wk,
           xa_wv, xa_wo, norm_final):
    b, s, d = x.shape
    m = mem.shape[1]
    depth = w_in.shape[0]
    tri = jnp.asarray(np.tril(np.ones((CHUNK, CHUNK), np.float32)), dtype=BF16)
    mem2d = mem.reshape(b * m, d)

    def row(p):
        return p[:, None, :]

    w_in_p = _pack_w_in(w_in)
    gup = jnp.concatenate(
        [gla_gate_up, jnp.zeros((depth, LANES - GLA_RANK, N_HEADS * GLA_DK), F32)], axis=1).astype(BF16)
    sbias = row(jnp.concatenate(
        [jnp.zeros((depth, GLA_RANK), F32), mlstm_igate_bias, mlstm_fgate_bias,
         jnp.zeros((depth, LANES - GLA_RANK - 2 * N_HEADS), F32)], axis=1))
    wqkv = _block_diag_dense(jnp.stack([mlstm_wq, mlstm_wk, mlstm_wv]))
    w_out_b, xa_wq_b, xa_wk_b, xa_wv_b, xa_wo_b = (
        t.astype(BF16) for t in (w_out, xa_wq, xa_wk, xa_wv, xa_wo))

    for l in range(depth):
        proj = _inproj(x.reshape(b * s, d), row(norm_mix), w_in_p, layer=l, tm=min(512, b * s))
        mix = _mixers(
            proj.reshape(b, s, PROJ_COLS), tri, hgrn_lb_logits, row(hgrn_norm), gup,
            row(gla_gate_bias), row(gla_norm), mlstm_conv_w, row(mlstm_conv_b), wqkv, sbias,
            row(mlstm_skip), row(mlstm_norm), layer=l)
        kmem, vmem = _memkv(mem2d, row(norm_mem), xa_wk_b, xa_wv_b, layer=l, tm=256)
        x = _outattn(mix, x, w_out_b, row(norm_xattn), xa_wq_b, kmem.reshape(b, m, d),
                     vmem.reshape(b, m, d), xa_wo_b, norm_final[None, :], layer=l,
                     tm=min(512, s), final_norm=(l == depth - 1))
    return x
```

```python
import functools
import math

import numpy as np
import jax
import jax.numpy as jnp
from jax import lax
from jax.experimental import pallas as pl
from jax.experimental.pallas import tpu as pltpu

F32 = jnp.float32
BF16 = jnp.bfloat16

RMS_EPS = 1e-6
D_MODEL = 1024
CHUNK = 64
STEP_CHUNKS = 4
STEP_SEQS = 2
SUB = 8
LANES = 128
N_HEADS = 4
HGRN_D = 128
GLA_DK = 64
GLA_DV = 128
GLA_RANK = 16
GLA_GATE_NORMALIZER = 16.0
MLSTM_DH = 128
MLSTM_CONV = 4
XA_HEAD_DIM = D_MODEL // N_HEADS
MIX_WIDTH = 3 * N_HEADS * 128

C_AQ, C_AF, C_AI, C_AZ = 0, 512, 1024, 1536
C_GQ, C_GK, C_GV, C_GZ = 2048, 2304, 2560, 3072
C_MU, C_MZ = 3584, 4096
C_SMALL = 4608
W_COLS = C_SMALL + LANES
C_HK = W_COLS
C_GLF = C_HK + 512
PROJ_COLS = C_GLF + 256
L_MI = GLA_RANK
L_MF = GLA_RANK + N_HEADS

N_LEVELS = 6
LOG_ZERO = -1e30
LOG2E = math.log2(math.e)
VMEM_LIMIT = 48 * 1024 * 1024


def _exp_neg(x):
    return jnp.exp2(x * (-LOG2E))


def _sigmoid(x):
    return 1.0 / (1.0 + _exp_neg(x))


def _silu(x):
    return x * _sigmoid(x)


def _log_sigmoid(x):
    return jnp.minimum(x, 0.0) - jnp.log(1.0 + _exp_neg(jnp.abs(x)))


def _dot(a, b):
    return jnp.dot(a, b, preferred_element_type=F32)


def _dot_nt(a, b):
    return lax.dot_general(a, b, (((1,), (1,)), ((), ())), preferred_element_type=F32)


def _dot_tn(a, b):
    return lax.dot_general(a, b, (((0,), (0,)), ((), ())), preferred_element_type=F32)


def _rms_norm(x, g):
    return x * lax.rsqrt(jnp.mean(x * x, axis=-1, keepdims=True) + RMS_EPS) * g


def _split3(x):
    hi = x.astype(BF16)
    r1 = x - hi.astype(F32)
    mid = r1.astype(BF16)
    lo = (r1 - mid.astype(F32)).astype(BF16)
    return hi, mid, lo


def _cumsum_rows(tri_bf16, x):
    hi = x.astype(BF16)
    lo = (x - hi.astype(F32)).astype(BF16)
    return _dot(tri_bf16, hi) + _dot(tri_bf16, lo)


def _layer_spec(layer, shape):
    return pl.BlockSpec((None,) + tuple(shape), lambda *_: (layer,) + (0,) * len(shape))


def _hgrn_lower_bound(lb_logit_ref, layer):
    logits = lb_logit_ref[...]
    ex = jnp.exp(logits - jnp.max(logits, axis=0, keepdims=True))
    soft = ex / jnp.sum(ex, axis=0, keepdims=True)
    c0 = soft[0:1, :]
    cl = c0
    for r in range(1, layer + 1):
        cl = cl + soft[r:r + 1, :]
    return cl - c0


def _inproj_kernel(x_ref, g_ref, w_ref, lb_logit_ref, gup_ref, gbias_ref, sbias_ref, o_ref, *, layer):
    tm = x_ref.shape[0]
    h = _rms_norm(x_ref[...], g_ref[...]).astype(BF16)

    def cols(c0, c1):
        return _dot(h, w_ref[:, c0:c1])

    lb = _hgrn_lower_bound(lb_logit_ref, layer)
    one_m_lb = 1.0 - lb
    o_ref[:, C_AQ:C_AF] = _silu(cols(C_AQ, C_AF))
    sig_f = _sigmoid(cols(C_AF, C_AI))
    o_ref[:, C_AF:C_AI] = jnp.maximum(jnp.log(lb + one_m_lb * sig_f) * LOG2E, LOG_ZERO)
    o_ref[:, C_HK:C_GLF] = one_m_lb * (1.0 - sig_f)
    o_ref[:, C_AI:C_AZ] = cols(C_AI, C_AZ)
    o_ref[:, C_AZ:C_GQ] = _silu(cols(C_AZ, C_GQ))
    o_ref[:, C_GQ:C_GK] = cols(C_GQ, C_GK) * (GLA_DK ** -0.5)
    o_ref[:, C_GK:C_GZ] = cols(C_GK, C_GZ)
    o_ref[:, C_GZ:C_MU] = _silu(cols(C_GZ, C_MU))
    o_ref[:, C_MU:C_MZ] = cols(C_MU, C_MZ)
    o_ref[:, C_MZ:C_SMALL] = _silu(cols(C_MZ, C_SMALL))
    small = cols(C_SMALL, W_COLS)
    gla_pre = _dot(small.astype(BF16), gup_ref[...]) + gbias_ref[...]
    o_ref[:, C_GLF:PROJ_COLS] = _log_sigmoid(gla_pre) * (LOG2E / GLA_GATE_NORMALIZER)
    lane = lax.broadcasted_iota(jnp.int32, (tm, LANES), 1)
    sb = small + sbias_ref[...]
    is_f = jnp.logical_and(lane >= L_MF, lane < L_MF + N_HEADS)
    is_i = jnp.logical_and(lane >= L_MI, lane < L_MI + N_HEADS)
    o_ref[:, C_SMALL:W_COLS] = jnp.where(is_i, sb, jnp.where(is_f, _log_sigmoid(sb), 0.0))


def _inproj(x2d, g, w, lb_logits, gup, gbias, sbias, layer, tm):
    n = x2d.shape[0]
    depth = lb_logits.shape[0]
    return pl.pallas_call(
        functools.partial(_inproj_kernel, layer=layer),
        out_shape=jax.ShapeDtypeStruct((n, PROJ_COLS), F32),
        grid=(n // tm,),
        in_specs=[
            pl.BlockSpec((tm, D_MODEL), lambda i: (i, 0)),
            _layer_spec(layer, (1, D_MODEL)),
            _layer_spec(layer, (D_MODEL, W_COLS)),
            pl.BlockSpec((depth, N_HEADS * HGRN_D), lambda i: (0, 0)),
            _layer_spec(layer, (LANES, N_HEADS * GLA_DK)),
            _layer_spec(layer, (1, N_HEADS * GLA_DK)),
            _layer_spec(layer, (1, LANES)),
        ],
        out_specs=pl.BlockSpec((tm, PROJ_COLS), lambda i: (i, 0)),
        compiler_params=pltpu.CompilerParams(
            dimension_semantics=("arbitrary",), vmem_limit_bytes=VMEM_LIMIT),
        name="inproj",
    )(x2d, g, w, lb_logits, gup, gbias, sbias)


def _memkv_kernel(m_ref, g_ref, wk_ref, wv_ref, k_ref, v_ref):
    h = _rms_norm(m_ref[...], g_ref[...]).astype(BF16)
    k_ref[...] = _dot(h, wk_ref[...]).astype(BF16)
    v_ref[...] = _dot(h, wv_ref[...]).astype(BF16)


def _memkv(mem2d, g, wk, wv, layer, tm):
    n = mem2d.shape[0]
    full = _layer_spec(layer, (D_MODEL, D_MODEL))
    tile = pl.BlockSpec((tm, D_MODEL), lambda i: (i, 0))
    return pl.pallas_call(
        _memkv_kernel,
        out_shape=(jax.ShapeDtypeStruct((n, D_MODEL), BF16),) * 2,
        grid=(n // tm,),
        in_specs=[tile, _layer_spec(layer, (1, D_MODEL)), full, full],
        out_specs=(tile, tile),
        compiler_params=pltpu.CompilerParams(
            dimension_semantics=("arbitrary",), vmem_limit_bytes=VMEM_LIMIT),
        name="memkv",
    )(mem2d, g, wk, wv)


def _row_signs(w):
    row = lax.broadcasted_iota(jnp.int32, (CHUNK // SUB, SUB, w), 1)
    sgn4 = jnp.where(row >= 4, 1.0, -1.0)
    sgn2 = jnp.where((row & 3) >= 2, 1.0, -1.0)
    odd = jnp.where((row & 1) == 1, 1.0, 0.0)
    return row < 4, sgn4, sgn2, odd


def _decay_exponents(b, lf, signs):
    w = b.shape[-1]
    groups = CHUNK // SUB
    first4, sgn4, sgn2, odd = (t[:, :, :w] for t in signs)
    b3 = b.reshape(groups, SUB, w)
    levels = []
    for m in (32, 16, 8):
        gm = m // SUB
        parts = []
        for blk in range(groups // (2 * gm)):
            g0 = blk * 2 * gm
            ref = b3[g0 + gm - 1:g0 + gm, SUB - 1:SUB, :]
            parts.append(ref - b3[g0:g0 + gm])
            parts.append(b3[g0 + gm:g0 + 2 * gm] - ref)
        levels.append(jnp.concatenate(parts, axis=0))
    levels.append((b3 - b3[:, 3:4, :]) * sgn4)
    levels.append((b3 - jnp.where(first4, b3[:, 1:2, :], b3[:, 5:6, :])) * sgn2)
    levels.append(lf.reshape(groups, SUB, w) * odd)
    e_kstate = b3[groups - 1:groups, SUB - 1:SUB, :] - b3
    return [e.reshape(CHUNK, w) for e in levels], e_kstate.reshape(CHUNK, w)


def _pair_masks():
    i = lax.broadcasted_iota(jnp.int32, (CHUNK, LANES), 0)
    j = lax.broadcasted_iota(jnp.int32, (CHUNK, LANES), 1) & (CHUNK - 1)
    x = jnp.bitwise_xor(i, j)
    below = i > j
    masks = [jnp.where(i == j, 1.0, 0.0)]
    for lvl in range(N_LEVELS):
        top_bit = N_LEVELS - 1 - lvl
        masks.append(jnp.where(jnp.logical_and(below, jnp.right_shift(x, top_bit) == 1), 1.0, 0.0))
    return masks


def _block_rows(t0, t1):
    z = jnp.zeros_like(t0)
    return jnp.concatenate([jnp.concatenate([t0, z], axis=1), jnp.concatenate([z, t1], axis=1)], axis=0)


def _pair_scores(q16, levels, masks, key_blocks):
    scores = _dot_nt(q16, key_blocks(None)) * masks[0]
    for lvl in range(N_LEVELS):
        ee = jnp.exp2(levels[lvl]).astype(BF16)
        scores = scores + _dot_nt(q16 * ee, key_blocks(ee)) * masks[1 + lvl]
    return scores


def _lockstep(gens):
    live = list(gens)
    while live:
        live = [g for g in live if next(g, StopIteration) is not StopIteration]


def _mixer_kernel(proj_ref, tri_ref, convw_ref, convb_ref, wqkv_ref, skip_ref, mnorm_ref,
                  out_ref,
                  hst_ref, gst_ref, mc_ref, mn_ref, mm_ref, conv_ref):
    nsq = proj_ref.shape[0]

    @pl.when(pl.program_id(1) == 0)
    def _():
        hst_ref[...] = jnp.zeros_like(hst_ref)
        gst_ref[...] = jnp.zeros_like(gst_ref)
        mc_ref[...] = jnp.zeros_like(mc_ref)
        mn_ref[...] = jnp.zeros_like(mn_ref)
        mm_ref[...] = jnp.zeros_like(mm_ref)
        conv_ref[:, 0:SUB, :] = jnp.zeros((nsq, SUB, N_HEADS * MLSTM_DH), F32)

    tri = tri_ref[...]
    masks = _pair_masks()
    signs = _row_signs(2 * HGRN_D)
    lane = lax.broadcasted_iota(jnp.int32, (CHUNK, LANES), 1)
    low_half = lane < GLA_DK
    is_i = jnp.logical_and(lane >= L_MI, lane < L_MI + N_HEADS)
    lane_st = lax.broadcasted_iota(jnp.int32, (GLA_DV, LANES), 1)
    eye = (lax.broadcasted_iota(jnp.int32, (LANES, LANES), 0)
           == lax.broadcasted_iota(jnp.int32, (LANES, LANES), 1)).astype(BF16)
    ci = lax.broadcasted_iota(jnp.int32, (CHUNK, CHUNK), 0)
    cj = lax.broadcasted_iota(jnp.int32, (CHUNK, CHUNK), 1)
    causal = ci >= cj

    def chunk(sq, r0):
        rows = slice(r0, r0 + CHUNK)

        small = proj_ref[sq, rows, C_SMALL:C_SMALL + LANES]
        h_lf = proj_ref[sq, rows, C_AF:C_AF + N_HEADS * HGRN_D]
        h_k = proj_ref[sq, rows, C_HK:C_HK + N_HEADS * HGRN_D]
        m_lf = jnp.where(is_i, 0.0, small)

        b_hm = _cumsum_rows(tri, jnp.concatenate([h_lf, m_lf], axis=1))
        h_b = b_hm[:, :N_HEADS * HGRN_D]
        m_b = b_hm[:, N_HEADS * HGRN_D:]
        yield

        m_u = proj_ref[sq, rows, C_MU:C_MU + N_HEADS * MLSTM_DH]
        conv_ref[sq, SUB:SUB + CHUNK, :] = m_u
        acc = convb_ref[...]
        for tap in range(MLSTM_CONV):
            off = SUB - (MLSTM_CONV - 1) + tap
            acc = acc + conv_ref[sq, off:off + CHUNK, :] * convw_ref[tap:tap + 1, :]
        conv_ref[sq, 0:SUB, :] = m_u[CHUNK - SUB:CHUNK, :]
        conv = _silu(acc)
        m_q, m_k, m_v = [], [], []
        for h in range(N_HEADS):
            sl = slice(h * MLSTM_DH, (h + 1) * MLSTM_DH)
            cb = conv[:, sl].astype(BF16)
            m_q.append(_dot(cb, wqkv_ref[0, h]))
            m_k.append(_dot(cb, wqkv_ref[1, h]) * (MLSTM_DH ** -0.5))
            m_v.append(_dot(m_u[:, sl].astype(BF16), wqkv_ref[2, h]).astype(BF16))
        yield

        z = jnp.where(is_i, small, m_b)
        zh, zm, zl = _split3(z)
        z_t = _dot_nt(eye, zh) + _dot_nt(eye, zm) + _dot_nt(eye, zl)

        g_lf = proj_ref[sq, rows, C_GLF:C_GLF + N_HEADS * GLA_DK]
        g_b = _cumsum_rows(tri, g_lf)
        yield

        h_q = proj_ref[sq, rows, C_AQ:C_AQ + N_HEADS * HGRN_D]
        h_scores, h_vb = [], []
        for pair in range(N_HEADS // 2):
            psl = slice(pair * 2 * HGRN_D, (pair + 1) * 2 * HGRN_D)
            q, k, b = h_q[:, psl], h_k[:, psl], h_b[:, psl]
            levels, e_k = _decay_exponents(b, h_lf[:, psl], signs)
            q_inter = (q * jnp.exp2(b)).astype(BF16)
            k_state = (k * jnp.exp2(e_k)).astype(BF16)
            decay = jnp.exp2(b[CHUNK - 1:CHUNK, :])
            vb = proj_ref[sq, rows,
                          C_AI + pair * 2 * HGRN_D:C_AI + (pair + 1) * 2 * HGRN_D].astype(BF16)
            o_inter = []
            for hp in range(2):
                h = 2 * pair + hp
                hsl = slice(hp * HGRN_D, (hp + 1) * HGRN_D)
                st = hst_ref[sq, h]
                o_inter.append(_dot_nt(q_inter[:, hsl], st.astype(BF16)))
                hst_ref[sq, h] = decay[:, hsl] * st + _dot_tn(vb[:, hsl], k_state[:, hsl])
            k16 = k.astype(BF16)

            def h_keys(scale, k16=k16):
                t = k16 if scale is None else k16 * scale
                return _block_rows(t[:, :HGRN_D], t[:, HGRN_D:])

            h_scores.append((jnp.concatenate(o_inter, axis=1),
                             _pair_scores(q.astype(BF16), levels, masks, h_keys)))
            h_vb.append(vb)
            yield

        m_parts = []
        for h in range(N_HEADS):
            q, k, vb = m_q[h], m_k[h], m_v[h]
            qb, kb = q.astype(BF16), k.astype(BF16)
            bc = z[:, L_MF + h:L_MF + h + 1]
            ic = z[:, L_MI + h:L_MI + h + 1]
            br = z_t[L_MF + h:L_MF + h + 1, :]
            ir = z_t[L_MI + h:L_MI + h + 1, :]
            m_prev = mm_ref[sq, h:h + 1, 0:1]
            c_prev = mc_ref[sq, h]
            n_prev = mn_ref[sq, h:h + 1, :]

            log_d = jnp.where(causal, bc - br + ir, -jnp.inf)
            m_inter = bc + m_prev
            m_i = jnp.maximum(m_inter, jnp.max(log_d, axis=-1, keepdims=True))
            w_inter = jnp.exp(m_inter - m_i)
            s = _dot_nt(qb, kb) * jnp.exp(log_d - m_i)
            num_inter = w_inter * _dot(qb, c_prev.astype(BF16))
            den = (w_inter * jnp.sum(q * n_prev, axis=-1, keepdims=True)
                   + jnp.sum(s, axis=-1, keepdims=True))
            scale = 1.0 / jnp.maximum(jnp.abs(den), jnp.exp(-m_i))

            b_last = bc[CHUNK - 1:CHUNK, :]
            log_w = b_last - bc + ic
            m_new = jnp.maximum(b_last + m_prev, jnp.max(log_w, axis=0, keepdims=True))
            w_prev = jnp.exp(b_last + m_prev - m_new)
            kw = jnp.exp(log_w - m_new) * k
            mc_ref[sq, h] = w_prev * c_prev + _dot_tn(kw.astype(BF16), vb)
            mn_ref[sq, h:h + 1, :] = w_prev * n_prev + jnp.sum(kw, axis=0, keepdims=True)
            mm_ref[sq, h:h + 1, :] = jnp.broadcast_to(m_new, (1, LANES))
            m_parts.append((num_inter, s.astype(BF16), scale))
        yield

        g_scores, g_vb = [], []
        for pair in range(N_HEADS // 2):
            psl = slice(pair * LANES, (pair + 1) * LANES)
            q = proj_ref[sq, rows, C_GQ + pair * LANES:C_GQ + (pair + 1) * LANES]
            k = proj_ref[sq, rows, C_GK + pair * LANES:C_GK + (pair + 1) * LANES]
            b = g_b[:, psl]
            levels, e_k = _decay_exponents(b, g_lf[:, psl], signs)
            q_inter = q * jnp.exp2(b)
            k_state = (k * jnp.exp2(e_k)).astype(BF16)
            decay = jnp.exp2(b[CHUNK - 1:CHUNK, :])
            vb = proj_ref[sq, rows,
                          C_GV + pair * 2 * GLA_DV:C_GV + (pair + 1) * 2 * GLA_DV].astype(BF16)
            st = gst_ref[sq, pair]
            stb = st.astype(BF16)
            o_inter = [_dot_nt(jnp.where(low_half, q_inter, 0.0).astype(BF16), stb),
                       _dot_nt(jnp.where(low_half, 0.0, q_inter).astype(BF16), stb)]
            gst_ref[sq, pair] = decay * st + jnp.where(lane_st < GLA_DK,
                                                       _dot_tn(vb[:, :GLA_DV], k_state),
                                                       _dot_tn(vb[:, GLA_DV:], k_state))
            k_lo = jnp.where(low_half, k, 0.0).astype(BF16)
            k_hi = jnp.where(low_half, 0.0, k).astype(BF16)

            def g_keys(scale, k_lo=k_lo, k_hi=k_hi):
                if scale is None:
                    return jnp.concatenate([k_lo, k_hi], axis=0)
                return jnp.concatenate([k_lo * scale, k_hi * scale], axis=0)

            g_scores.append((jnp.concatenate(o_inter, axis=1),
                             _pair_scores(q.astype(BF16), levels, masks, g_keys)))
            g_vb.append(vb)
            yield

        for pair in range(N_HEADS // 2):
            o_inter, scores = h_scores[pair]
            vb = h_vb[pair]
            o = o_inter + _dot(scores.astype(BF16), _block_rows(vb[:, :HGRN_D], vb[:, HGRN_D:]))
            out_ref[sq, rows, pair * 2 * HGRN_D:(pair + 1) * 2 * HGRN_D] = o.astype(out_ref.dtype)
        yield

        for h in range(N_HEADS):
            sl = slice(h * MLSTM_DH, (h + 1) * MLSTM_DH)
            num_inter, sbf, scale = m_parts[h]
            hh = (num_inter + _dot(sbf, m_v[h])) * scale
            mu = jnp.mean(hh, axis=-1, keepdims=True)
            hc = hh - mu
            y = hc * lax.rsqrt(jnp.mean(hc * hc, axis=-1, keepdims=True) + RMS_EPS) * mnorm_ref[:, sl]
            y = (y + skip_ref[:, sl] * conv[:, sl]) * proj_ref[
                sq, rows, C_MZ + h * MLSTM_DH:C_MZ + (h + 1) * MLSTM_DH]
            out_ref[sq, rows, 1024 + h * 128:1024 + (h + 1) * 128] = y.astype(out_ref.dtype)
        yield

        for pair in range(N_HEADS // 2):
            o_inter, scores = g_scores[pair]
            vb = g_vb[pair]
            o = o_inter + _dot(scores.astype(BF16), _block_rows(vb[:, :GLA_DV], vb[:, GLA_DV:]))
            out_ref[sq, rows, 512 + pair * 2 * GLA_DV:512 + (pair + 1) * 2 * GLA_DV] = o.astype(
                out_ref.dtype)

    for c in range(proj_ref.shape[1] // CHUNK):
        _lockstep([chunk(sq, c * CHUNK) for sq in range(nsq)])


def _mixers(proj, tri, convw, convb, wqkv, skip, mnorm, layer):
    b, s, _ = proj.shape
    t = min(CHUNK * STEP_CHUNKS, s)
    nsq = STEP_SEQS if b % STEP_SEQS == 0 else 1

    def whole(shape):
        return pl.BlockSpec(shape, lambda i, j: (0,) * len(shape))

    return pl.pallas_call(
        _mixer_kernel,
        out_shape=jax.ShapeDtypeStruct((b, s, MIX_WIDTH), BF16),
        grid=(b // nsq, s // t),
        in_specs=[
            pl.BlockSpec((nsq, t, PROJ_COLS), lambda i, j: (i, j, 0)),
            whole((CHUNK, CHUNK)),
            _layer_spec(layer, (MLSTM_CONV, N_HEADS * MLSTM_DH)),
            _layer_spec(layer, (1, N_HEADS * MLSTM_DH)),
            pl.BlockSpec((3, None, N_HEADS, MLSTM_DH, MLSTM_DH), lambda i, j: (0, layer, 0, 0, 0)),
            _layer_spec(layer, (1, N_HEADS * MLSTM_DH)),
            _layer_spec(layer, (1, N_HEADS * MLSTM_DH)),
        ],
        out_specs=pl.BlockSpec((nsq, t, MIX_WIDTH), lambda i, j: (i, j, 0)),
        scratch_shapes=[
            pltpu.VMEM((nsq, N_HEADS, HGRN_D, HGRN_D), F32),
            pltpu.VMEM((nsq, N_HEADS // 2, GLA_DV, LANES), F32),
            pltpu.VMEM((nsq, N_HEADS, MLSTM_DH, MLSTM_DH), F32),
            pltpu.VMEM((nsq, SUB, MLSTM_DH), F32),
            pltpu.VMEM((nsq, SUB, LANES), F32),
            pltpu.VMEM((nsq, SUB + CHUNK, N_HEADS * MLSTM_DH), F32),
        ],
        compiler_params=pltpu.CompilerParams(
            dimension_semantics=("arbitrary", "arbitrary"), vmem_limit_bytes=VMEM_LIMIT),
        name="mixers",
    )(proj, tri, convw, convb, wqkv, skip, mnorm)


def _outattn_kernel(mix_ref, az_ref, gz_ref, x_ref, hnorm_ref, gnorm_ref, wout_ref, gx_ref, wq_ref,
                    k_ref, v_ref, wo_ref, gf_ref, o_ref, *, final_norm):
    tm = x_ref.shape[0]
    halves = [slice(a * (tm // 2), (a + 1) * (tm // 2)) for a in range(2)] if tm % 16 == 0 else [slice(0, tm)]

    def mixed(r):
        parts = []
        for c0, gate_ref, g_ref in ((0, az_ref, hnorm_ref), (N_HEADS * HGRN_D, gz_ref, gnorm_ref)):
            for h in range(N_HEADS):
                oh = mix_ref[r, c0 + h * 128:c0 + (h + 1) * 128].astype(F32)
                y = oh * lax.rsqrt(jnp.mean(oh * oh, axis=-1, keepdims=True) + RMS_EPS) * g_ref[...]
                parts.append((y * gate_ref[r, h * 128:(h + 1) * 128]).astype(BF16))
        parts.append(mix_ref[r, 2 * N_HEADS * 128:])
        return jnp.concatenate(parts, axis=-1)

    x1 = [x_ref[r, :] + _dot(mixed(r), wout_ref[...]) for r in halves]
    q = [_dot(_rms_norm(t, gx_ref[...]).astype(BF16), wq_ref[...]).astype(BF16) for t in x1]
    items = [(a, hd) for a in range(len(halves)) for hd in range(N_HEADS)]

    def scores(item):
        a, hd = item
        sl = slice(hd * XA_HEAD_DIM, (hd + 1) * XA_HEAD_DIM)
        return _dot_nt(q[a][:, sl], k_ref[:, sl]) * (XA_HEAD_DIM ** -0.5)

    lookahead = 2
    pending = [scores(it) for it in items[:lookahead]]
    outs = [[] for _ in halves]
    for n, (a, hd) in enumerate(items):
        s = pending.pop(0)
        e = jnp.exp(s - jnp.max(s, axis=-1, keepdims=True))
        p = e / jnp.sum(e, axis=-1, keepdims=True)
        sl = slice(hd * XA_HEAD_DIM, (hd + 1) * XA_HEAD_DIM)
        outs[a].append(_dot(p.astype(BF16), v_ref[:, sl]).astype(BF16))
        if n + lookahead < len(items):
            pending.append(scores(items[n + lookahead]))
    for a, r in enumerate(halves):
        x2 = x1[a] + _dot(jnp.concatenate(outs[a], axis=-1), wo_ref[...])
        if final_norm:
            x2 = _rms_norm(x2, gf_ref[...])
        o_ref[r, :] = x2


def _outattn(mix, proj, x, hnorm, gnorm, wout, gx, wq, kmem, vmem, wo, gf, layer, tm, final_norm):
    b, s, _ = x.shape
    m = kmem.shape[1]
    gate_w = N_HEADS * 128

    return pl.pallas_call(
        functools.partial(_outattn_kernel, final_norm=final_norm),
        out_shape=jax.ShapeDtypeStruct((b, s, D_MODEL), F32),
        grid=(b, s // tm),
        in_specs=[
            pl.BlockSpec((None, tm, MIX_WIDTH), lambda i, j: (i, j, 0)),
            pl.BlockSpec((None, tm, gate_w), lambda i, j: (i, j, C_AZ // gate_w)),
            pl.BlockSpec((None, tm, gate_w), lambda i, j: (i, j, C_GZ // gate_w)),
            pl.BlockSpec((None, tm, D_MODEL), lambda i, j: (i, j, 0)),
            _layer_spec(layer, (1, HGRN_D)),
            _layer_spec(layer, (1, GLA_DV)),
            _layer_spec(layer, (MIX_WIDTH, D_MODEL)),
            _layer_spec(layer, (1, D_MODEL)),
            _layer_spec(layer, (D_MODEL, D_MODEL)),
            pl.BlockSpec((None, m, D_MODEL), lambda i, j: (i, 0, 0)),
            pl.BlockSpec((None, m, D_MODEL), lambda i, j: (i, 0, 0)),
            _layer_spec(layer, (D_MODEL, D_MODEL)),
            pl.BlockSpec((1, D_MODEL), lambda i, j: (0, 0)),
        ],
        out_specs=pl.BlockSpec((None, tm, D_MODEL), lambda i, j: (i, j, 0)),
        compiler_params=pltpu.CompilerParams(
            dimension_semantics=("arbitrary", "arbitrary"), vmem_limit_bytes=VMEM_LIMIT),
        name="outattn",
    )(mix, proj, proj, x, hnorm, gnorm, wout, gx, wq, kmem, vmem, wo, gf)


def _pack_w_in(w):
    ga0 = 2048 + 2 * N_HEADS * GLA_DK + N_HEADS * GLA_DV
    ga1 = ga0 + GLA_RANK
    gate0 = ga1 + N_HEADS * GLA_DV + 2 * N_HEADS * MLSTM_DH
    pad = jnp.zeros(w.shape[:-1] + (LANES - GLA_RANK - 2 * N_HEADS,), w.dtype)
    return jnp.concatenate([w[..., :ga0], w[..., ga1:gate0], w[..., ga0:ga1], w[..., gate0:], pad],
                           axis=-1).astype(BF16)


def _block_diag_dense(w):
    lead = w.shape[:-3]
    blk = w.shape[-1]
    rows = w.reshape(lead + (N_HEADS, MLSTM_DH, blk))
    tiled = jnp.tile(rows, (1,) * (len(lead) + 2) + (MLSTM_DH // blk,))
    r = np.arange(MLSTM_DH)
    on_diag = jnp.asarray((r[:, None] // blk) == (r[None, :] // blk))
    return jnp.where(on_diag, tiled, 0.0).astype(BF16)


def kernel(x, mem, norm_mix, w_in, hgrn_lb_logits, hgrn_norm, gla_gate_up, gla_gate_bias, gla_norm,
           mlstm_conv_w, mlstm_conv_b, mlstm_wq, mlstm_wk, mlstm_wv, mlstm_igate_bias,
           mlstm_fgate_bias, mlstm_skip, mlstm_norm, w_out, norm_xattn, norm_mem, xa_wq, xa_wk,
           xa_wv, xa_wo, norm_final):
    b, s, d = x.shape
    m = mem.shape[1]
    depth = w_in.shape[0]
    tri = jnp.asarray(np.tril(np.ones((CHUNK, CHUNK), np.float32)), dtype=BF16)
    mem2d = mem.reshape(b * m, d)

    def row(p):
        return p[:, None, :]

    w_in_p = _pack_w_in(w_in)
    gup = jnp.concatenate(
        [gla_gate_up, jnp.zeros((depth, LANES - GLA_RANK, N_HEADS * GLA_DK), F32)], axis=1).astype(BF16)
    sbias = row(jnp.concatenate(
        [jnp.zeros((depth, GLA_RANK), F32), mlstm_igate_bias, mlstm_fgate_bias,
         jnp.zeros((depth, LANES - GLA_RANK - 2 * N_HEADS), F32)], axis=1))
    wqkv = _block_diag_dense(jnp.stack([mlstm_wq, mlstm_wk, mlstm_wv]))
    w_out_b, xa_wq_b, xa_wk_b, xa_wv_b, xa_wo_b = (
        t.astype(BF16) for t in (w_out, xa_wq, xa_wk, xa_wv, xa_wo))

    for l in range(depth):
        proj = _inproj(x.reshape(b * s, d), row(norm_mix), w_in_p, hgrn_lb_logits, gup,
                       row(gla_gate_bias), sbias, layer=l,
                       tm=min(512, b * s)).reshape(b, s, PROJ_COLS)
        mix = _mixers(proj, tri, mlstm_conv_w, row(mlstm_conv_b), wqkv, row(mlstm_skip),
                      row(mlstm_norm), layer=l)
        kmem, vmem = _memkv(mem2d, row(norm_mem), xa_wk_b, xa_wv_b, layer=l, tm=256)
        x = _outattn(mix, proj, x, row(hgrn_norm), row(gla_norm), w_out_b, row(norm_xattn), xa_wq_b,
                     kmem.reshape(b, m, d), vmem.reshape(b, m, d), xa_wo_b, norm_final[None, :],
                     layer=l, tm=min(512, s), final_norm=(l == depth - 1))
    return x
```

```python
import functools
import math

import numpy as np
import jax
import jax.numpy as jnp
from jax import lax
from jax.experimental import pallas as pl
from jax.experimental.pallas import tpu as pltpu

F32 = jnp.float32
BF16 = jnp.bfloat16

RMS_EPS = 1e-6
D_MODEL = 1024
CHUNK = 64
STEP_CHUNKS = 4
STEP_SEQS = 2
SUB = 8
LANES = 128
N_HEADS = 4
HGRN_D = 128
GLA_DK = 64
GLA_DV = 128
GLA_RANK = 16
GLA_GATE_NORMALIZER = 16.0
MLSTM_DH = 128
MLSTM_CONV = 4
XA_HEAD_DIM = D_MODEL // N_HEADS
MIX_WIDTH = 3 * N_HEADS * 128
O_MLSTM = 2 * N_HEADS * 128
O_CONV = MIX_WIDTH
MIXOUT_WIDTH = MIX_WIDTH + N_HEADS * MLSTM_DH

C_AQ, C_AF, C_AI, C_AZ = 0, 512, 1024, 1536
C_GQ, C_GK, C_GV, C_GZ = 2048, 2304, 2560, 3072
C_MU, C_MZ = 3584, 4096
C_SMALL = 4608
W_COLS = C_SMALL + LANES
C_HK = W_COLS
C_GLF = C_HK + 512
PROJ_COLS = C_GLF + 256
L_MI = GLA_RANK
L_MF = GLA_RANK + N_HEADS

N_LEVELS = 6
LOG_ZERO = -1e30
LOG2E = math.log2(math.e)
VMEM_LIMIT = 48 * 1024 * 1024
OUTATTN_VMEM_LIMIT = 56 * 1024 * 1024


def _exp_neg(x):
    return jnp.exp2(x * (-LOG2E))


def _sigmoid(x):
    return 1.0 / (1.0 + _exp_neg(x))


def _silu(x):
    return x * _sigmoid(x)


def _log_sigmoid(x):
    return jnp.minimum(x, 0.0) - jnp.log(1.0 + _exp_neg(jnp.abs(x)))


def _dot(a, b):
    return jnp.dot(a, b, preferred_element_type=F32)


def _dot_nt(a, b):
    return lax.dot_general(a, b, (((1,), (1,)), ((), ())), preferred_element_type=F32)


def _dot_tn(a, b):
    return lax.dot_general(a, b, (((0,), (0,)), ((), ())), preferred_element_type=F32)


def _rms_norm(x, g):
    return x * lax.rsqrt(jnp.mean(x * x, axis=-1, keepdims=True) + RMS_EPS) * g


def _split3(x):
    hi = x.astype(BF16)
    r1 = x - hi.astype(F32)
    mid = r1.astype(BF16)
    lo = (r1 - mid.astype(F32)).astype(BF16)
    return hi, mid, lo


def _cumsum_rows(tri_bf16, x):
    hi = x.astype(BF16)
    lo = (x - hi.astype(F32)).astype(BF16)
    return _dot(tri_bf16, hi) + _dot(tri_bf16, lo)


def _layer_spec(layer, shape):
    return pl.BlockSpec((None,) + tuple(shape), lambda *_: (layer,) + (0,) * len(shape))


def _hgrn_lower_bound(lb_logit_ref, layer):
    logits = lb_logit_ref[...]
    ex = jnp.exp(logits - jnp.max(logits, axis=0, keepdims=True))
    soft = ex / jnp.sum(ex, axis=0, keepdims=True)
    c0 = soft[0:1, :]
    cl = c0
    for r in range(1, layer + 1):
        cl = cl + soft[r:r + 1, :]
    return cl - c0


def _inproj_kernel(x_ref, g_ref, wa_ref, wb_ref, lb_logit_ref, gup_ref, gbias_ref, sbias_ref, o_ref,
                   *, layer):
    tm = x_ref.shape[0]
    h = _rms_norm(x_ref[...], g_ref[...]).astype(BF16)

    def cols(c0, c1):
        if c1 <= C_GZ:
            return _dot(h, wa_ref[:, c0:c1])
        return _dot(h, wb_ref[:, c0 - C_GZ:c1 - C_GZ])

    lb = _hgrn_lower_bound(lb_logit_ref, layer)
    one_m_lb = 1.0 - lb
    o_ref[:, C_AQ:C_AF] = _silu(cols(C_AQ, C_AF))
    sig_f = _sigmoid(cols(C_AF, C_AI))
    o_ref[:, C_AF:C_AI] = jnp.maximum(jnp.log(lb + one_m_lb * sig_f) * LOG2E, LOG_ZERO)
    o_ref[:, C_HK:C_GLF] = one_m_lb * (1.0 - sig_f)
    o_ref[:, C_AI:C_AZ] = cols(C_AI, C_AZ)
    o_ref[:, C_AZ:C_GQ] = _silu(cols(C_AZ, C_GQ))
    o_ref[:, C_GQ:C_GK] = cols(C_GQ, C_GK) * (GLA_DK ** -0.5)
    o_ref[:, C_GK:C_GZ] = cols(C_GK, C_GZ)
    o_ref[:, C_GZ:C_MU] = _silu(cols(C_GZ, C_MU))
    o_ref[:, C_MU:C_MZ] = cols(C_MU, C_MZ)
    o_ref[:, C_MZ:C_SMALL] = _silu(cols(C_MZ, C_SMALL))
    small = cols(C_SMALL, W_COLS)
    gla_pre = _dot(small.astype(BF16), gup_ref[...]) + gbias_ref[...]
    o_ref[:, C_GLF:PROJ_COLS] = _log_sigmoid(gla_pre) * (LOG2E / GLA_GATE_NORMALIZER)
    lane = lax.broadcasted_iota(jnp.int32, (tm, LANES), 1)
    sb = small + sbias_ref[...]
    is_f = jnp.logical_and(lane >= L_MF, lane < L_MF + N_HEADS)
    is_i = jnp.logical_and(lane >= L_MI, lane < L_MI + N_HEADS)
    o_ref[:, C_SMALL:W_COLS] = jnp.where(is_i, sb, jnp.where(is_f, _log_sigmoid(sb), 0.0))


def _inproj(x2d, g, wa, wb, lb_logits, gup, gbias, sbias, layer, tm):
    n = x2d.shape[0]
    depth = lb_logits.shape[0]
    return pl.pallas_call(
        functools.partial(_inproj_kernel, layer=layer),
        out_shape=jax.ShapeDtypeStruct((n, PROJ_COLS), F32),
        grid=(n // tm,),
        in_specs=[
            pl.BlockSpec((tm, D_MODEL), lambda i: (i, 0)),
            _layer_spec(layer, (1, D_MODEL)),
            _layer_spec(layer, (D_MODEL, C_GZ)),
            _layer_spec(layer, (D_MODEL, W_COLS - C_GZ)),
            pl.BlockSpec((depth, N_HEADS * HGRN_D), lambda i: (0, 0)),
            _layer_spec(layer, (LANES, N_HEADS * GLA_DK)),
            _layer_spec(layer, (1, N_HEADS * GLA_DK)),
            _layer_spec(layer, (1, LANES)),
        ],
        out_specs=pl.BlockSpec((tm, PROJ_COLS), lambda i: (i, 0)),
        compiler_params=pltpu.CompilerParams(
            dimension_semantics=("arbitrary",), vmem_limit_bytes=VMEM_LIMIT),
        name="inproj",
    )(x2d, g, wa, wb, lb_logits, gup, gbias, sbias)


def _memkv_kernel(m_ref, g_ref, wk_ref, wv_ref, k_ref, v_ref):
    h = _rms_norm(m_ref[...], g_ref[...]).astype(BF16)
    k_ref[...] = _dot(h, wk_ref[...]).astype(BF16)
    v_ref[...] = _dot(h, wv_ref[...]).astype(BF16)


def _memkv(mem2d, g, wk, wv, layer, tm):
    n = mem2d.shape[0]
    full = _layer_spec(layer, (D_MODEL, D_MODEL))
    tile = pl.BlockSpec((tm, D_MODEL), lambda i: (i, 0))
    return pl.pallas_call(
        _memkv_kernel,
        out_shape=(jax.ShapeDtypeStruct((n, D_MODEL), BF16),) * 2,
        grid=(n // tm,),
        in_specs=[tile, _layer_spec(layer, (1, D_MODEL)), full, full],
        out_specs=(tile, tile),
        compiler_params=pltpu.CompilerParams(
            dimension_semantics=("arbitrary",), vmem_limit_bytes=VMEM_LIMIT),
        name="memkv",
    )(mem2d, g, wk, wv)


def _row_signs(w):
    row = lax.broadcasted_iota(jnp.int32, (CHUNK // SUB, SUB, w), 1)
    sgn4 = jnp.where(row >= 4, 1.0, -1.0)
    sgn2 = jnp.where((row & 3) >= 2, 1.0, -1.0)
    odd = jnp.where((row & 1) == 1, 1.0, 0.0)
    return row < 4, sgn4, sgn2, odd


def _decay_exponents(b, lf, signs):
    w = b.shape[-1]
    groups = CHUNK // SUB
    first4, sgn4, sgn2, odd = (t[:, :, :w] for t in signs)
    b3 = b.reshape(groups, SUB, w)
    levels = []
    for m in (32, 16, 8):
        gm = m // SUB
        parts = []
        for blk in range(groups // (2 * gm)):
            g0 = blk * 2 * gm
            ref = b3[g0 + gm - 1:g0 + gm, SUB - 1:SUB, :]
            parts.append(ref - b3[g0:g0 + gm])
            parts.append(b3[g0 + gm:g0 + 2 * gm] - ref)
        levels.append(jnp.concatenate(parts, axis=0))
    levels.append((b3 - b3[:, 3:4, :]) * sgn4)
    levels.append((b3 - jnp.where(first4, b3[:, 1:2, :], b3[:, 5:6, :])) * sgn2)
    levels.append(lf.reshape(groups, SUB, w) * odd)
    e_kstate = b3[groups - 1:groups, SUB - 1:SUB, :] - b3
    return [e.reshape(CHUNK, w) for e in levels], e_kstate.reshape(CHUNK, w)


def _pair_masks():
    i = lax.broadcasted_iota(jnp.int32, (CHUNK, LANES), 0)
    j = lax.broadcasted_iota(jnp.int32, (CHUNK, LANES), 1) & (CHUNK - 1)
    x = jnp.bitwise_xor(i, j)
    below = i > j
    masks = [jnp.where(i == j, 1.0, 0.0)]
    for lvl in range(N_LEVELS):
        top_bit = N_LEVELS - 1 - lvl
        masks.append(jnp.where(jnp.logical_and(below, jnp.right_shift(x, top_bit) == 1), 1.0, 0.0))
    return masks


def _block_rows(t0, t1):
    z = jnp.zeros_like(t0)
    return jnp.concatenate([jnp.concatenate([t0, z], axis=1), jnp.concatenate([z, t1], axis=1)], axis=0)


def _pair_scores(q16, levels, masks, key_blocks):
    scores = _dot_nt(q16, key_blocks(None)) * masks[0]
    for lvl in range(N_LEVELS):
        ee = jnp.exp2(levels[lvl]).astype(BF16)
        scores = scores + _dot_nt(q16 * ee, key_blocks(ee)) * masks[1 + lvl]
    return scores


def _lockstep(gens):
    live = list(gens)
    while live:
        live = [g for g in live if next(g, StopIteration) is not StopIteration]


def _mixer_kernel(proj_ref, tri_ref, convw_ref, convb_ref, wqkv_ref,
                  out_ref,
                  hst_ref, gst_ref, mc_ref, mn_ref, mm_ref, conv_ref):
    nsq = proj_ref.shape[0]

    @pl.when(pl.program_id(1) == 0)
    def _():
        hst_ref[...] = jnp.zeros_like(hst_ref)
        gst_ref[...] = jnp.zeros_like(gst_ref)
        mc_ref[...] = jnp.zeros_like(mc_ref)
        mn_ref[...] = jnp.zeros_like(mn_ref)
        mm_ref[...] = jnp.zeros_like(mm_ref)
        conv_ref[:, 0:SUB, :] = jnp.zeros((nsq, SUB, N_HEADS * MLSTM_DH), F32)

    tri = tri_ref[...]
    masks = _pair_masks()
    signs = _row_signs(2 * HGRN_D)
    lane = lax.broadcasted_iota(jnp.int32, (CHUNK, LANES), 1)
    low_half = lane < GLA_DK
    is_i = jnp.logical_and(lane >= L_MI, lane < L_MI + N_HEADS)
    lane_st = lax.broadcasted_iota(jnp.int32, (GLA_DV, LANES), 1)
    eye = (lax.broadcasted_iota(jnp.int32, (LANES, LANES), 0)
           == lax.broadcasted_iota(jnp.int32, (LANES, LANES), 1)).astype(BF16)
    ci = lax.broadcasted_iota(jnp.int32, (CHUNK, CHUNK), 0)
    cj = lax.broadcasted_iota(jnp.int32, (CHUNK, CHUNK), 1)
    causal = ci >= cj

    def chunk(sq, r0):
        rows = slice(r0, r0 + CHUNK)

        small = proj_ref[sq, rows, C_SMALL:C_SMALL + LANES]
        h_lf = proj_ref[sq, rows, C_AF:C_AF + N_HEADS * HGRN_D]
        h_k = proj_ref[sq, rows, C_HK:C_HK + N_HEADS * HGRN_D]
        m_lf = jnp.where(is_i, 0.0, small)

        b_hm = _cumsum_rows(tri, jnp.concatenate([h_lf, m_lf], axis=1))
        h_b = b_hm[:, :N_HEADS * HGRN_D]
        m_b = b_hm[:, N_HEADS * HGRN_D:]
        yield

        m_u = proj_ref[sq, rows, C_MU:C_MU + N_HEADS * MLSTM_DH]
        conv_ref[sq, SUB:SUB + CHUNK, :] = m_u
        acc = convb_ref[...]
        for tap in range(MLSTM_CONV):
            off = SUB - (MLSTM_CONV - 1) + tap
            acc = acc + conv_ref[sq, off:off + CHUNK, :] * convw_ref[tap:tap + 1, :]
        conv_ref[sq, 0:SUB, :] = m_u[CHUNK - SUB:CHUNK, :]
        conv = _silu(acc)
        m_q, m_k, m_v = [], [], []
        for h in range(N_HEADS):
            sl = slice(h * MLSTM_DH, (h + 1) * MLSTM_DH)
            cb = conv[:, sl].astype(BF16)
            m_q.append(_dot(cb, wqkv_ref[0, h]))
            m_k.append(_dot(cb, wqkv_ref[1, h]) * (MLSTM_DH ** -0.5))
            m_v.append(_dot(m_u[:, sl].astype(BF16), wqkv_ref[2, h]).astype(BF16))
        yield

        z = jnp.where(is_i, small, m_b)
        zh, zm, zl = _split3(z)
        z_t = _dot_nt(eye, zh) + _dot_nt(eye, zm) + _dot_nt(eye, zl)

        g_lf = proj_ref[sq, rows, C_GLF:C_GLF + N_HEADS * GLA_DK]
        g_b = _cumsum_rows(tri, g_lf)
        yield

        h_q = proj_ref[sq, rows, C_AQ:C_AQ + N_HEADS * HGRN_D]
        h_scores, h_vb = [], []
        for pair in range(N_HEADS // 2):
            psl = slice(pair * 2 * HGRN_D, (pair + 1) * 2 * HGRN_D)
            q, k, b = h_q[:, psl], h_k[:, psl], h_b[:, psl]
            levels, e_k = _decay_exponents(b, h_lf[:, psl], signs)
            q_inter = (q * jnp.exp2(b)).astype(BF16)
            k_state = (k * jnp.exp2(e_k)).astype(BF16)
            decay = jnp.exp2(b[CHUNK - 1:CHUNK, :])
            vb = proj_ref[sq, rows,
                          C_AI + pair * 2 * HGRN_D:C_AI + (pair + 1) * 2 * HGRN_D].astype(BF16)
            o_inter = []
            for hp in range(2):
                h = 2 * pair + hp
                hsl = slice(hp * HGRN_D, (hp + 1) * HGRN_D)
                st = hst_ref[sq, h]
                o_inter.append(_dot_nt(q_inter[:, hsl], st.astype(BF16)))
                hst_ref[sq, h] = decay[:, hsl] * st + _dot_tn(vb[:, hsl], k_state[:, hsl])
            k16 = k.astype(BF16)

            def h_keys(scale, k16=k16):
                t = k16 if scale is None else k16 * scale
                return _block_rows(t[:, :HGRN_D], t[:, HGRN_D:])

            h_scores.append((jnp.concatenate(o_inter, axis=1),
                             _pair_scores(q.astype(BF16), levels, masks, h_keys)))
            h_vb.append(vb)
            yield

        m_parts = []
        for h in range(N_HEADS):
            q, k, vb = m_q[h], m_k[h], m_v[h]
            qb, kb = q.astype(BF16), k.astype(BF16)
            bc = z[:, L_MF + h:L_MF + h + 1]
            ic = z[:, L_MI + h:L_MI + h + 1]
            br = z_t[L_MF + h:L_MF + h + 1, :]
            ir = z_t[L_MI + h:L_MI + h + 1, :]
            m_prev = mm_ref[sq, h:h + 1, 0:1]
            c_prev = mc_ref[sq, h]
            n_prev = mn_ref[sq, h:h + 1, :]

            log_d = jnp.where(causal, bc - br + ir, -jnp.inf)
            m_inter = bc + m_prev
            m_i = jnp.maximum(m_inter, jnp.max(log_d, axis=-1, keepdims=True))
            w_inter = jnp.exp(m_inter - m_i)
            s = _dot_nt(qb, kb) * jnp.exp(log_d - m_i)
            num_inter = w_inter * _dot(qb, c_prev.astype(BF16))
            den = (w_inter * jnp.sum(q * n_prev, axis=-1, keepdims=True)
                   + jnp.sum(s, axis=-1, keepdims=True))
            scale = 1.0 / jnp.maximum(jnp.abs(den), jnp.exp(-m_i))

            b_last = bc[CHUNK - 1:CHUNK, :]
            log_w = b_last - bc + ic
            m_new = jnp.maximum(b_last + m_prev, jnp.max(log_w, axis=0, keepdims=True))
            w_prev = jnp.exp(b_last + m_prev - m_new)
            kw = jnp.exp(log_w - m_new) * k
            mc_ref[sq, h] = w_prev * c_prev + _dot_tn(kw.astype(BF16), vb)
            mn_ref[sq, h:h + 1, :] = w_prev * n_prev + jnp.sum(kw, axis=0, keepdims=True)
            mm_ref[sq, h:h + 1, :] = jnp.broadcast_to(m_new, (1, LANES))
            m_parts.append((num_inter, s.astype(BF16), scale))
        yield

        g_scores, g_vb = [], []
        for pair in range(N_HEADS // 2):
            psl = slice(pair * LANES, (pair + 1) * LANES)
            q = proj_ref[sq, rows, C_GQ + pair * LANES:C_GQ + (pair + 1) * LANES]
            k = proj_ref[sq, rows, C_GK + pair * LANES:C_GK + (pair + 1) * LANES]
            b = g_b[:, psl]
            levels, e_k = _decay_exponents(b, g_lf[:, psl], signs)
            q_inter = q * jnp.exp2(b)
            k_state = (k * jnp.exp2(e_k)).astype(BF16)
            decay = jnp.exp2(b[CHUNK - 1:CHUNK, :])
            vb = proj_ref[sq, rows,
                          C_GV + pair * 2 * GLA_DV:C_GV + (pair + 1) * 2 * GLA_DV].astype(BF16)
            st = gst_ref[sq, pair]
            stb = st.astype(BF16)
            o_inter = [_dot_nt(jnp.where(low_half, q_inter, 0.0).astype(BF16), stb),
                       _dot_nt(jnp.where(low_half, 0.0, q_inter).astype(BF16), stb)]
            gst_ref[sq, pair] = decay * st + jnp.where(lane_st < GLA_DK,
                                                       _dot_tn(vb[:, :GLA_DV], k_state),
                                                       _dot_tn(vb[:, GLA_DV:], k_state))
            k_lo = jnp.where(low_half, k, 0.0).astype(BF16)
            k_hi = jnp.where(low_half, 0.0, k).astype(BF16)

            def g_keys(scale, k_lo=k_lo, k_hi=k_hi):
                if scale is None:
                    return jnp.concatenate([k_lo, k_hi], axis=0)
                return jnp.concatenate([k_lo * scale, k_hi * scale], axis=0)

            g_scores.append((jnp.concatenate(o_inter, axis=1),
                             _pair_scores(q.astype(BF16), levels, masks, g_keys)))
            g_vb.append(vb)
            yield

        for pair in range(N_HEADS // 2):
            o_inter, scores = h_scores[pair]
            vb = h_vb[pair]
            o = o_inter + _dot(scores.astype(BF16), _block_rows(vb[:, :HGRN_D], vb[:, HGRN_D:]))
            out_ref[sq, rows, pair * 2 * HGRN_D:(pair + 1) * 2 * HGRN_D] = o.astype(out_ref.dtype)
        yield

        for h in range(N_HEADS):
            num_inter, sbf, scale = m_parts[h]
            hh = (num_inter + _dot(sbf, m_v[h])) * scale
            out_ref[sq, rows, O_MLSTM + h * 128:O_MLSTM + (h + 1) * 128] = hh.astype(out_ref.dtype)
        out_ref[sq, rows, O_CONV:MIXOUT_WIDTH] = conv.astype(out_ref.dtype)
        yield

        for pair in range(N_HEADS // 2):
            o_inter, scores = g_scores[pair]
            vb = g_vb[pair]
            o = o_inter + _dot(scores.astype(BF16), _block_rows(vb[:, :GLA_DV], vb[:, GLA_DV:]))
            out_ref[sq, rows, 512 + pair * 2 * GLA_DV:512 + (pair + 1) * 2 * GLA_DV] = o.astype(
                out_ref.dtype)

    for c in range(proj_ref.shape[1] // CHUNK):
        _lockstep([chunk(sq, c * CHUNK) for sq in range(nsq)])


def _mixers(proj, tri, convw, convb, wqkv, layer):
    b, s, _ = proj.shape
    t = min(CHUNK * STEP_CHUNKS, s)
    nsq = STEP_SEQS if b % STEP_SEQS == 0 else 1

    def whole(shape):
        return pl.BlockSpec(shape, lambda i, j: (0,) * len(shape))

    return pl.pallas_call(
        _mixer_kernel,
        out_shape=jax.ShapeDtypeStruct((b, s, MIXOUT_WIDTH), BF16),
        grid=(b // nsq, s // t),
        in_specs=[
            pl.BlockSpec((nsq, t, PROJ_COLS), lambda i, j: (i, j, 0)),
            whole((CHUNK, CHUNK)),
            _layer_spec(layer, (MLSTM_CONV, N_HEADS * MLSTM_DH)),
            _layer_spec(layer, (1, N_HEADS * MLSTM_DH)),
            pl.BlockSpec((3, None, N_HEADS, MLSTM_DH, MLSTM_DH), lambda i, j: (0, layer, 0, 0, 0)),
        ],
        out_specs=pl.BlockSpec((nsq, t, MIXOUT_WIDTH), lambda i, j: (i, j, 0)),
        scratch_shapes=[
            pltpu.VMEM((nsq, N_HEADS, HGRN_D, HGRN_D), F32),
            pltpu.VMEM((nsq, N_HEADS // 2, GLA_DV, LANES), F32),
            pltpu.VMEM((nsq, N_HEADS, MLSTM_DH, MLSTM_DH), F32),
            pltpu.VMEM((nsq, SUB, MLSTM_DH), F32),
            pltpu.VMEM((nsq, SUB, LANES), F32),
            pltpu.VMEM((nsq, SUB + CHUNK, N_HEADS * MLSTM_DH), F32),
        ],
        compiler_params=pltpu.CompilerParams(
            dimension_semantics=("arbitrary", "arbitrary"), vmem_limit_bytes=VMEM_LIMIT),
        name="mixers",
    )(proj, tri, convw, convb, wqkv)


def _outattn_kernel(mix_ref, az_ref, gz_ref, mz_ref, x_ref, hnorm_ref, gnorm_ref, mnorm_ref, skip_ref,
                    wout_ref, gx_ref, wq_ref, k_ref, v_ref, wo_ref, gf_ref, o_ref, *, final_norm):
    tm = x_ref.shape[0]
    halves = [slice(a * (tm // 4), (a + 1) * (tm // 4)) for a in range(4)] if tm % 64 == 0 else [slice(0, tm)]

    def mixed(r):
        parts = []
        for c0, gate_ref, g_ref in ((0, az_ref, hnorm_ref), (N_HEADS * HGRN_D, gz_ref, gnorm_ref)):
            for h in range(N_HEADS):
                oh = mix_ref[r, c0 + h * 128:c0 + (h + 1) * 128].astype(F32)
                y = oh * lax.rsqrt(jnp.mean(oh * oh, axis=-1, keepdims=True) + RMS_EPS) * g_ref[...]
                parts.append((y * gate_ref[r, h * 128:(h + 1) * 128]).astype(BF16))
        for h in range(N_HEADS):
            sl = slice(h * MLSTM_DH, (h + 1) * MLSTM_DH)
            hh = mix_ref[r, O_MLSTM + h * MLSTM_DH:O_MLSTM + (h + 1) * MLSTM_DH].astype(F32)
            conv = mix_ref[r, O_CONV + h * MLSTM_DH:O_CONV + (h + 1) * MLSTM_DH].astype(F32)
            hc = hh - jnp.mean(hh, axis=-1, keepdims=True)
            y = hc * lax.rsqrt(jnp.mean(hc * hc, axis=-1, keepdims=True) + RMS_EPS) * mnorm_ref[:, sl]
            parts.append(((y + skip_ref[:, sl] * conv) * mz_ref[r, sl]).astype(BF16))
        return jnp.concatenate(parts, axis=-1)

    x1 = [x_ref[r, :] + _dot(mixed(r), wout_ref[...]) for r in halves]
    q = [_dot(_rms_norm(t, gx_ref[...]).astype(BF16), wq_ref[...]).astype(BF16) for t in x1]
    items = [(a, hd) for a in range(len(halves)) for hd in range(N_HEADS)]

    def scores(item):
        a, hd = item
        sl = slice(hd * XA_HEAD_DIM, (hd + 1) * XA_HEAD_DIM)
        return _dot_nt(q[a][:, sl], k_ref[:, sl]) * (XA_HEAD_DIM ** -0.5)

    lookahead = 2
    pending = [scores(it) for it in items[:lookahead]]
    outs = [[] for _ in halves]
    for n, (a, hd) in enumerate(items):
        s = pending.pop(0)
        e = jnp.exp(s - jnp.max(s, axis=-1, keepdims=True))
        p = e / jnp.sum(e, axis=-1, keepdims=True)
        sl = slice(hd * XA_HEAD_DIM, (hd + 1) * XA_HEAD_DIM)
        outs[a].append(_dot(p.astype(BF16), v_ref[:, sl]).astype(BF16))
        if n + lookahead < len(items):
            pending.append(scores(items[n + lookahead]))
    for a, r in enumerate(halves):
        x2 = x1[a] + _dot(jnp.concatenate(outs[a], axis=-1), wo_ref[...])
        if final_norm:
            x2 = _rms_norm(x2, gf_ref[...])
        o_ref[r, :] = x2


def _outattn(mix, proj, x, hnorm, gnorm, mnorm, skip, wout, gx, wq, kmem, vmem, wo, gf, layer, tm,
             final_norm):
    b, s, _ = x.shape
    m = kmem.shape[1]
    gate_w = N_HEADS * 128

    return pl.pallas_call(
        functools.partial(_outattn_kernel, final_norm=final_norm),
        out_shape=jax.ShapeDtypeStruct((b, s, D_MODEL), F32),
        grid=(b, s // tm),
        in_specs=[
            pl.BlockSpec((None, tm, MIXOUT_WIDTH), lambda i, j: (i, j, 0)),
            pl.BlockSpec((None, tm, gate_w), lambda i, j: (i, j, C_AZ // gate_w)),
            pl.BlockSpec((None, tm, gate_w), lambda i, j: (i, j, C_GZ // gate_w)),
            pl.BlockSpec((None, tm, gate_w), lambda i, j: (i, j, C_MZ // gate_w)),
            pl.BlockSpec((None, tm, D_MODEL), lambda i, j: (i, j, 0)),
            _layer_spec(layer, (1, HGRN_D)),
            _layer_spec(layer, (1, GLA_DV)),
            _layer_spec(layer, (1, N_HEADS * MLSTM_DH)),
            _layer_spec(layer, (1, N_HEADS * MLSTM_DH)),
            _layer_spec(layer, (MIX_WIDTH, D_MODEL)),
            _layer_spec(layer, (1, D_MODEL)),
            _layer_spec(layer, (D_MODEL, D_MODEL)),
            pl.BlockSpec((None, m, D_MODEL), lambda i, j: (i, 0, 0)),
            pl.BlockSpec((None, m, D_MODEL), lambda i, j: (i, 0, 0)),
            _layer_spec(layer, (D_MODEL, D_MODEL)),
            pl.BlockSpec((1, D_MODEL), lambda i, j: (0, 0)),
        ],
        out_specs=pl.BlockSpec((None, tm, D_MODEL), lambda i, j: (i, j, 0)),
        compiler_params=pltpu.CompilerParams(
            dimension_semantics=("arbitrary", "arbitrary"), vmem_limit_bytes=OUTATTN_VMEM_LIMIT),
        name="outattn",
    )(mix, proj, proj, proj, x, hnorm, gnorm, mnorm, skip, wout, gx, wq, kmem, vmem, wo, gf)


def _pack_w_in(w):
    ga0 = 2048 + 2 * N_HEADS * GLA_DK + N_HEADS * GLA_DV
    ga1 = ga0 + GLA_RANK
    gate0 = ga1 + N_HEADS * GLA_DV + 2 * N_HEADS * MLSTM_DH
    assert ga0 == C_GZ
    pad = jnp.zeros(w.shape[:-1] + (LANES - GLA_RANK - 2 * N_HEADS,), w.dtype)
    tail = jnp.concatenate([w[..., ga1:gate0], w[..., ga0:ga1], w[..., gate0:], pad], axis=-1)
    return w[..., :ga0].astype(BF16), tail.astype(BF16)


def _block_diag_dense(w):
    lead = w.shape[:-3]
    blk = w.shape[-1]
    rows = w.reshape(lead + (N_HEADS, MLSTM_DH, blk))
    tiled = jnp.tile(rows, (1,) * (len(lead) + 2) + (MLSTM_DH // blk,))
    r = np.arange(MLSTM_DH)
    on_diag = jnp.asarray((r[:, None] // blk) == (r[None, :] // blk))
    return jnp.where(on_diag, tiled, 0.0).astype(BF16)


def kernel(x, mem, norm_mix, w_in, hgrn_lb_logits, hgrn_norm, gla_gate_up, gla_gate_bias, gla_norm,
           mlstm_conv_w, mlstm_conv_b, mlstm_wq, mlstm_wk, mlstm_wv, mlstm_igate_bias,
           mlstm_fgate_bias, mlstm_skip, mlstm_norm, w_out, norm_xattn, norm_mem, xa_wq, xa_wk,
           xa_wv, xa_wo, norm_final):
    b, s, d = x.shape
    m = mem.shape[1]
    depth = w_in.shape[0]
    tri = jnp.asarray(np.tril(np.ones((CHUNK, CHUNK), np.float32)), dtype=BF16)
    mem2d = mem.reshape(b * m, d)

    def row(p):
        return p[:, None, :]

    w_in_a, w_in_b = _pack_w_in(w_in)
    gup = jnp.concatenate(
        [gla_gate_up, jnp.zeros((depth, LANES - GLA_RANK, N_HEADS * GLA_DK), F32)], axis=1).astype(BF16)
    sbias = row(jnp.concatenate(
        [jnp.zeros((depth, GLA_RANK), F32), mlstm_igate_bias, mlstm_fgate_bias,
         jnp.zeros((depth, LANES - GLA_RANK - 2 * N_HEADS), F32)], axis=1))
    wqkv = _block_diag_dense(jnp.stack([mlstm_wq, mlstm_wk, mlstm_wv]))
    w_out_b, xa_wq_b, xa_wk_b, xa_wv_b, xa_wo_b = (
        t.astype(BF16) for t in (w_out, xa_wq, xa_wk, xa_wv, xa_wo))

    for l in range(depth):
        proj = _inproj(x.reshape(b * s, d), row(norm_mix), w_in_a, w_in_b, hgrn_lb_logits, gup,
                       row(gla_gate_bias), sbias, layer=l,
                       tm=min(512, b * s)).reshape(b, s, PROJ_COLS)
        mix = _mixers(proj, tri, mlstm_conv_w, row(mlstm_conv_b), wqkv, layer=l)
        kmem, vmem = _memkv(mem2d, row(norm_mem), xa_wk_b, xa_wv_b, layer=l, tm=256)
        x = _outattn(mix, proj, x, row(hgrn_norm), row(gla_norm), row(mlstm_norm), row(mlstm_skip),
                     w_out_b, row(norm_xattn), xa_wq_b, kmem.reshape(b, m, d), vmem.reshape(b, m, d),
                     xa_wo_b, norm_final[None, :], layer=l, tm=min(1024, s),
                     final_norm=(l == depth - 1))
    return x
```

```python
import functools
import math

import numpy as np
import jax
import jax.numpy as jnp
from jax import lax
from jax.experimental import pallas as pl
from jax.experimental.pallas import tpu as pltpu

F32 = jnp.float32
BF16 = jnp.bfloat16

RMS_EPS = 1e-6
D_MODEL = 1024
CHUNK = 64
STEP_CHUNKS = 4
STEP_SEQS = 2
CHUNK_LAG = 3
SUB = 8
LANES = 128
N_HEADS = 4
HGRN_D = 128
GLA_DK = 64
GLA_DV = 128
GLA_RANK = 16
GLA_GATE_NORMALIZER = 16.0
MLSTM_DH = 128
MLSTM_CONV = 4
XA_HEAD_DIM = D_MODEL // N_HEADS
MIX_WIDTH = 3 * N_HEADS * 128
O_MLSTM = 2 * N_HEADS * 128
O_CONV = MIX_WIDTH
MIXOUT_WIDTH = MIX_WIDTH + N_HEADS * MLSTM_DH

C_AQ, C_AF, C_AI, C_AZ = 0, 512, 1024, 1536
C_GQ, C_GK, C_GV, C_GZ = 2048, 2304, 2560, 3072
C_MU, C_MZ = 3584, 4096
C_SMALL = 4608
W_COLS = C_SMALL + LANES
C_HK = W_COLS
C_GLF = C_HK + 512
PROJ_COLS = C_GLF + 256
L_MI = GLA_RANK
L_MF = GLA_RANK + N_HEADS

N_LEVELS = 6
LOG_ZERO = -1e30
LOG2E = math.log2(math.e)
VMEM_LIMIT = 48 * 1024 * 1024
OUTATTN_VMEM_LIMIT = 56 * 1024 * 1024


def _exp_neg(x):
    return jnp.exp2(x * (-LOG2E))


def _sigmoid(x):
    return 1.0 / (1.0 + _exp_neg(x))


def _silu(x):
    return x * _sigmoid(x)


def _log_sigmoid(x):
    return jnp.minimum(x, 0.0) - jnp.log(1.0 + _exp_neg(jnp.abs(x)))


def _dot(a, b):
    return jnp.dot(a, b, preferred_element_type=F32)


def _dot_nt(a, b):
    return lax.dot_general(a, b, (((1,), (1,)), ((), ())), preferred_element_type=F32)


def _dot_tn(a, b):
    return lax.dot_general(a, b, (((0,), (0,)), ((), ())), preferred_element_type=F32)


def _rms_norm(x, g):
    return x * lax.rsqrt(jnp.mean(x * x, axis=-1, keepdims=True) + RMS_EPS) * g


def _split3(x):
    hi = x.astype(BF16)
    r1 = x - hi.astype(F32)
    mid = r1.astype(BF16)
    lo = (r1 - mid.astype(F32)).astype(BF16)
    return hi, mid, lo


def _cumsum_rows(tri_bf16, x):
    hi = x.astype(BF16)
    lo = (x - hi.astype(F32)).astype(BF16)
    return _dot(tri_bf16, hi) + _dot(tri_bf16, lo)


def _layer_spec(layer, shape):
    return pl.BlockSpec((None,) + tuple(shape), lambda *_: (layer,) + (0,) * len(shape))


def _hgrn_lower_bound(lb_logit_ref, layer):
    logits = lb_logit_ref[...]
    ex = jnp.exp(logits - jnp.max(logits, axis=0, keepdims=True))
    soft = ex / jnp.sum(ex, axis=0, keepdims=True)
    c0 = soft[0:1, :]
    cl = c0
    for r in range(1, layer + 1):
        cl = cl + soft[r:r + 1, :]
    return cl - c0


def _inproj_kernel(x_ref, g_ref, wa_ref, wb_ref, lb_logit_ref, gup_ref, gbias_ref, sbias_ref, o_ref,
                   *, layer):
    tm = x_ref.shape[0]
    h = _rms_norm(x_ref[...], g_ref[...]).astype(BF16)

    def cols(c0, c1):
        if c1 <= C_GZ:
            return _dot(h, wa_ref[:, c0:c1])
        return _dot(h, wb_ref[:, c0 - C_GZ:c1 - C_GZ])

    lb = _hgrn_lower_bound(lb_logit_ref, layer)
    one_m_lb = 1.0 - lb
    o_ref[:, C_AQ:C_AF] = _silu(cols(C_AQ, C_AF))
    sig_f = _sigmoid(cols(C_AF, C_AI))
    o_ref[:, C_AF:C_AI] = jnp.maximum(jnp.log(lb + one_m_lb * sig_f) * LOG2E, LOG_ZERO)
    o_ref[:, C_HK:C_GLF] = one_m_lb * (1.0 - sig_f)
    o_ref[:, C_AI:C_AZ] = cols(C_AI, C_AZ)
    o_ref[:, C_AZ:C_GQ] = _silu(cols(C_AZ, C_GQ))
    o_ref[:, C_GQ:C_GK] = cols(C_GQ, C_GK) * (GLA_DK ** -0.5)
    o_ref[:, C_GK:C_GZ] = cols(C_GK, C_GZ)
    o_ref[:, C_GZ:C_MU] = _silu(cols(C_GZ, C_MU))
    o_ref[:, C_MU:C_MZ] = cols(C_MU, C_MZ)
    o_ref[:, C_MZ:C_SMALL] = _silu(cols(C_MZ, C_SMALL))
    small = cols(C_SMALL, W_COLS)
    gla_pre = _dot(small.astype(BF16), gup_ref[...]) + gbias_ref[...]
    o_ref[:, C_GLF:PROJ_COLS] = _log_sigmoid(gla_pre) * (LOG2E / GLA_GATE_NORMALIZER)
    lane = lax.broadcasted_iota(jnp.int32, (tm, LANES), 1)
    sb = small + sbias_ref[...]
    is_f = jnp.logical_and(lane >= L_MF, lane < L_MF + N_HEADS)
    is_i = jnp.logical_and(lane >= L_MI, lane < L_MI + N_HEADS)
    o_ref[:, C_SMALL:W_COLS] = jnp.where(is_i, sb, jnp.where(is_f, _log_sigmoid(sb), 0.0))


def _inproj(x2d, g, wa, wb, lb_logits, gup, gbias, sbias, layer, tm):
    n = x2d.shape[0]
    depth = lb_logits.shape[0]
    return pl.pallas_call(
        functools.partial(_inproj_kernel, layer=layer),
        out_shape=jax.ShapeDtypeStruct((n, PROJ_COLS), F32),
        grid=(n // tm,),
        in_specs=[
            pl.BlockSpec((tm, D_MODEL), lambda i: (i, 0)),
            _layer_spec(layer, (1, D_MODEL)),
            _layer_spec(layer, (D_MODEL, C_GZ)),
            _layer_spec(layer, (D_MODEL, W_COLS - C_GZ)),
            pl.BlockSpec((depth, N_HEADS * HGRN_D), lambda i: (0, 0)),
            _layer_spec(layer, (LANES, N_HEADS * GLA_DK)),
            _layer_spec(layer, (1, N_HEADS * GLA_DK)),
            _layer_spec(layer, (1, LANES)),
        ],
        out_specs=pl.BlockSpec((tm, PROJ_COLS), lambda i: (i, 0)),
        compiler_params=pltpu.CompilerParams(
            dimension_semantics=("arbitrary",), vmem_limit_bytes=VMEM_LIMIT),
        name="inproj",
    )(x2d, g, wa, wb, lb_logits, gup, gbias, sbias)


def _memkv_kernel(m_ref, g_ref, wk_ref, wv_ref, k_ref, v_ref):
    h = _rms_norm(m_ref[...], g_ref[...]).astype(BF16)
    k_ref[...] = _dot(h, wk_ref[...]).astype(BF16)
    v_ref[...] = _dot(h, wv_ref[...]).astype(BF16)


def _memkv(mem2d, g, wk, wv, layer, tm):
    n = mem2d.shape[0]
    full = _layer_spec(layer, (D_MODEL, D_MODEL))
    tile = pl.BlockSpec((tm, D_MODEL), lambda i: (i, 0))
    return pl.pallas_call(
        _memkv_kernel,
        out_shape=(jax.ShapeDtypeStruct((n, D_MODEL), BF16),) * 2,
        grid=(n // tm,),
        in_specs=[tile, _layer_spec(layer, (1, D_MODEL)), full, full],
        out_specs=(tile, tile),
        compiler_params=pltpu.CompilerParams(
            dimension_semantics=("arbitrary",), vmem_limit_bytes=VMEM_LIMIT),
        name="memkv",
    )(mem2d, g, wk, wv)


def _row_signs(w):
    row = lax.broadcasted_iota(jnp.int32, (CHUNK // SUB, SUB, w), 1)
    sgn4 = jnp.where(row >= 4, 1.0, -1.0)
    sgn2 = jnp.where((row & 3) >= 2, 1.0, -1.0)
    odd = jnp.where((row & 1) == 1, 1.0, 0.0)
    return row < 4, sgn4, sgn2, odd


def _decay_exponents(b, lf, signs):
    w = b.shape[-1]
    groups = CHUNK // SUB
    first4, sgn4, sgn2, odd = (t[:, :, :w] for t in signs)
    b3 = b.reshape(groups, SUB, w)
    levels = []
    for m in (32, 16, 8):
        gm = m // SUB
        parts = []
        for blk in range(groups // (2 * gm)):
            g0 = blk * 2 * gm
            ref = b3[g0 + gm - 1:g0 + gm, SUB - 1:SUB, :]
            parts.append(ref - b3[g0:g0 + gm])
            parts.append(b3[g0 + gm:g0 + 2 * gm] - ref)
        levels.append(jnp.concatenate(parts, axis=0))
    levels.append((b3 - b3[:, 3:4, :]) * sgn4)
    levels.append((b3 - jnp.where(first4, b3[:, 1:2, :], b3[:, 5:6, :])) * sgn2)
    levels.append(lf.reshape(groups, SUB, w) * odd)
    e_kstate = b3[groups - 1:groups, SUB - 1:SUB, :] - b3
    return [e.reshape(CHUNK, w) for e in levels], e_kstate.reshape(CHUNK, w)


def _pair_masks():
    i = lax.broadcasted_iota(jnp.int32, (CHUNK, LANES), 0)
    j = lax.broadcasted_iota(jnp.int32, (CHUNK, LANES), 1) & (CHUNK - 1)
    x = jnp.bitwise_xor(i, j)
    below = i > j
    masks = [jnp.where(i == j, 1.0, 0.0)]
    for lvl in range(N_LEVELS):
        top_bit = N_LEVELS - 1 - lvl
        masks.append(jnp.where(jnp.logical_and(below, jnp.right_shift(x, top_bit) == 1), 1.0, 0.0))
    return masks


def _block_rows(t0, t1):
    z = jnp.zeros_like(t0)
    return jnp.concatenate([jnp.concatenate([t0, z], axis=1), jnp.concatenate([z, t1], axis=1)], axis=0)


def _pair_scores(q16, levels, masks, key_blocks):
    scores = _dot_nt(q16, key_blocks(None)) * masks[0]
    for lvl in range(N_LEVELS):
        ee = jnp.exp2(levels[lvl]).astype(BF16)
        scores = scores + _dot_nt(q16 * ee, key_blocks(ee)) * masks[1 + lvl]
    return scores


def _lockstep(gens):
    live = list(gens)
    rnd = 0
    while live:
        live = [(s, g) for s, g in live
                if s > rnd or next(g, StopIteration) is not StopIteration]
        rnd += 1


def _mixer_kernel(proj_ref, tri_ref, convw_ref, convb_ref, wqkv_ref,
                  out_ref,
                  hst_ref, gst_ref, mc_ref, mn_ref, mm_ref, conv_ref):
    nsq = proj_ref.shape[0]

    @pl.when(pl.program_id(1) == 0)
    def _():
        hst_ref[...] = jnp.zeros_like(hst_ref)
        gst_ref[...] = jnp.zeros_like(gst_ref)
        mc_ref[...] = jnp.zeros_like(mc_ref)
        mn_ref[...] = jnp.zeros_like(mn_ref)
        mm_ref[...] = jnp.zeros_like(mm_ref)
        conv_ref[:, 0:SUB, :] = jnp.zeros((nsq, SUB, N_HEADS * MLSTM_DH), F32)

    tri = tri_ref[...]
    masks = _pair_masks()
    signs = _row_signs(2 * HGRN_D)
    lane = lax.broadcasted_iota(jnp.int32, (CHUNK, LANES), 1)
    low_half = lane < GLA_DK
    is_i = jnp.logical_and(lane >= L_MI, lane < L_MI + N_HEADS)
    lane_st = lax.broadcasted_iota(jnp.int32, (GLA_DV, LANES), 1)
    eye = (lax.broadcasted_iota(jnp.int32, (LANES, LANES), 0)
           == lax.broadcasted_iota(jnp.int32, (LANES, LANES), 1)).astype(BF16)
    ci = lax.broadcasted_iota(jnp.int32, (CHUNK, CHUNK), 0)
    cj = lax.broadcasted_iota(jnp.int32, (CHUNK, CHUNK), 1)
    causal = ci >= cj

    def chunk(sq, r0):
        rows = slice(r0, r0 + CHUNK)

        small = proj_ref[sq, rows, C_SMALL:C_SMALL + LANES]
        h_lf = proj_ref[sq, rows, C_AF:C_AF + N_HEADS * HGRN_D]
        h_k = proj_ref[sq, rows, C_HK:C_HK + N_HEADS * HGRN_D]
        m_lf = jnp.where(is_i, 0.0, small)

        b_hm = _cumsum_rows(tri, jnp.concatenate([h_lf, m_lf], axis=1))
        h_b = b_hm[:, :N_HEADS * HGRN_D]
        m_b = b_hm[:, N_HEADS * HGRN_D:]
        yield

        m_u = proj_ref[sq, rows, C_MU:C_MU + N_HEADS * MLSTM_DH]
        conv_ref[sq, SUB:SUB + CHUNK, :] = m_u
        acc = convb_ref[...]
        for tap in range(MLSTM_CONV):
            off = SUB - (MLSTM_CONV - 1) + tap
            acc = acc + conv_ref[sq, off:off + CHUNK, :] * convw_ref[tap:tap + 1, :]
        conv_ref[sq, 0:SUB, :] = m_u[CHUNK - SUB:CHUNK, :]
        conv = _silu(acc)
        m_q, m_k, m_v = [], [], []
        for h in range(N_HEADS):
            sl = slice(h * MLSTM_DH, (h + 1) * MLSTM_DH)
            cb = conv[:, sl].astype(BF16)
            m_q.append(_dot(cb, wqkv_ref[0, h]))
            m_k.append(_dot(cb, wqkv_ref[1, h]) * (MLSTM_DH ** -0.5))
            m_v.append(_dot(m_u[:, sl].astype(BF16), wqkv_ref[2, h]).astype(BF16))
        yield

        z = jnp.where(is_i, small, m_b)
        zh, zm, zl = _split3(z)
        z_t = _dot_nt(eye, zh) + _dot_nt(eye, zm) + _dot_nt(eye, zl)

        g_lf = proj_ref[sq, rows, C_GLF:C_GLF + N_HEADS * GLA_DK]
        g_b = _cumsum_rows(tri, g_lf)
        yield

        h_q = proj_ref[sq, rows, C_AQ:C_AQ + N_HEADS * HGRN_D]
        h_scores, h_vb = [], []
        for pair in range(N_HEADS // 2):
            psl = slice(pair * 2 * HGRN_D, (pair + 1) * 2 * HGRN_D)
            q, k, b = h_q[:, psl], h_k[:, psl], h_b[:, psl]
            levels, e_k = _decay_exponents(b, h_lf[:, psl], signs)
            q_inter = (q * jnp.exp2(b)).astype(BF16)
            k_state = (k * jnp.exp2(e_k)).astype(BF16)
            decay = jnp.exp2(b[CHUNK - 1:CHUNK, :])
            vb = proj_ref[sq, rows,
                          C_AI + pair * 2 * HGRN_D:C_AI + (pair + 1) * 2 * HGRN_D].astype(BF16)
            o_inter = []
            for hp in range(2):
                h = 2 * pair + hp
                hsl = slice(hp * HGRN_D, (hp + 1) * HGRN_D)
                st = hst_ref[sq, h]
                o_inter.append(_dot_nt(q_inter[:, hsl], st.astype(BF16)))
                hst_ref[sq, h] = decay[:, hsl] * st + _dot_tn(vb[:, hsl], k_state[:, hsl])
            k16 = k.astype(BF16)

            def h_keys(scale, k16=k16):
                t = k16 if scale is None else k16 * scale
                return _block_rows(t[:, :HGRN_D], t[:, HGRN_D:])

            h_scores.append((jnp.concatenate(o_inter, axis=1),
                             _pair_scores(q.astype(BF16), levels, masks, h_keys)))
            h_vb.append(vb)
            yield

        m_parts = []
        for h in range(N_HEADS):
            q, k, vb = m_q[h], m_k[h], m_v[h]
            qb, kb = q.astype(BF16), k.astype(BF16)
            bc = z[:, L_MF + h:L_MF + h + 1]
            ic = z[:, L_MI + h:L_MI + h + 1]
            br = z_t[L_MF + h:L_MF + h + 1, :]
            ir = z_t[L_MI + h:L_MI + h + 1, :]
            m_prev = mm_ref[sq, h:h + 1, 0:1]
            c_prev = mc_ref[sq, h]
            n_prev = mn_ref[sq, h:h + 1, :]

            log_d = jnp.where(causal, bc - br + ir, -jnp.inf)
            m_inter = bc + m_prev
            m_i = jnp.maximum(m_inter, jnp.max(log_d, axis=-1, keepdims=True))
            w_inter = jnp.exp(m_inter - m_i)
            s = _dot_nt(qb, kb) * jnp.exp(log_d - m_i)
            num_inter = w_inter * _dot(qb, c_prev.astype(BF16))
            den = (w_inter * jnp.sum(q * n_prev, axis=-1, keepdims=True)
                   + jnp.sum(s, axis=-1, keepdims=True))
            scale = 1.0 / jnp.maximum(jnp.abs(den), jnp.exp(-m_i))

            b_last = bc[CHUNK - 1:CHUNK, :]
            log_w = b_last - bc + ic
            m_new = jnp.maximum(b_last + m_prev, jnp.max(log_w, axis=0, keepdims=True))
            w_prev = jnp.exp(b_last + m_prev - m_new)
            kw = jnp.exp(log_w - m_new) * k
            mc_ref[sq, h] = w_prev * c_prev + _dot_tn(kw.astype(BF16), vb)
            mn_ref[sq, h:h + 1, :] = w_prev * n_prev + jnp.sum(kw, axis=0, keepdims=True)
            mm_ref[sq, h:h + 1, :] = jnp.broadcast_to(m_new, (1, LANES))
            m_parts.append((num_inter, s.astype(BF16), scale))
        yield

        g_scores, g_vb = [], []
        for pair in range(N_HEADS // 2):
            psl = slice(pair * LANES, (pair + 1) * LANES)
            q = proj_ref[sq, rows, C_GQ + pair * LANES:C_GQ + (pair + 1) * LANES]
            k = proj_ref[sq, rows, C_GK + pair * LANES:C_GK + (pair + 1) * LANES]
            b = g_b[:, psl]
            levels, e_k = _decay_exponents(b, g_lf[:, psl], signs)
            q_inter = q * jnp.exp2(b)
            k_state = (k * jnp.exp2(e_k)).astype(BF16)
            decay = jnp.exp2(b[CHUNK - 1:CHUNK, :])
            vb = proj_ref[sq, rows,
                          C_GV + pair * 2 * GLA_DV:C_GV + (pair + 1) * 2 * GLA_DV].astype(BF16)
            st = gst_ref[sq, pair]
            stb = st.astype(BF16)
            o_inter = [_dot_nt(jnp.where(low_half, q_inter, 0.0).astype(BF16), stb),
                       _dot_nt(jnp.where(low_half, 0.0, q_inter).astype(BF16), stb)]
            gst_ref[sq, pair] = decay * st + jnp.where(lane_st < GLA_DK,
                                                       _dot_tn(vb[:, :GLA_DV], k_state),
                                                       _dot_tn(vb[:, GLA_DV:], k_state))
            k_lo = jnp.where(low_half, k, 0.0).astype(BF16)
            k_hi = jnp.where(low_half, 0.0, k).astype(BF16)

            def g_keys(scale, k_lo=k_lo, k_hi=k_hi):
                if scale is None:
                    return jnp.concatenate([k_lo, k_hi], axis=0)
                return jnp.concatenate([k_lo * scale, k_hi * scale], axis=0)

            g_scores.append((jnp.concatenate(o_inter, axis=1),
                             _pair_scores(q.astype(BF16), levels, masks, g_keys)))
            g_vb.append(vb)
            yield

        for pair in range(N_HEADS // 2):
            o_inter, scores = h_scores[pair]
            vb = h_vb[pair]
            o = o_inter + _dot(scores.astype(BF16), _block_rows(vb[:, :HGRN_D], vb[:, HGRN_D:]))
            out_ref[sq, rows, pair * 2 * HGRN_D:(pair + 1) * 2 * HGRN_D] = o.astype(out_ref.dtype)
        yield

        for h in range(N_HEADS):
            num_inter, sbf, scale = m_parts[h]
            hh = (num_inter + _dot(sbf, m_v[h])) * scale
            out_ref[sq, rows, O_MLSTM + h * 128:O_MLSTM + (h + 1) * 128] = hh.astype(out_ref.dtype)
        out_ref[sq, rows, O_CONV:MIXOUT_WIDTH] = conv.astype(out_ref.dtype)
        yield

        for pair in range(N_HEADS // 2):
            o_inter, scores = g_scores[pair]
            vb = g_vb[pair]
            o = o_inter + _dot(scores.astype(BF16), _block_rows(vb[:, :GLA_DV], vb[:, GLA_DV:]))
            out_ref[sq, rows, 512 + pair * 2 * GLA_DV:512 + (pair + 1) * 2 * GLA_DV] = o.astype(
                out_ref.dtype)

    _lockstep([(c * CHUNK_LAG, chunk(sq, c * CHUNK))
               for c in range(proj_ref.shape[1] // CHUNK) for sq in range(nsq)])


def _mixers(proj, tri, convw, convb, wqkv, layer):
    b, s, _ = proj.shape
    t = min(CHUNK * STEP_CHUNKS, s)
    nsq = STEP_SEQS if b % STEP_SEQS == 0 else 1

    def whole(shape):
        return pl.BlockSpec(shape, lambda i, j: (0,) * len(shape))

    return pl.pallas_call(
        _mixer_kernel,
        out_shape=jax.ShapeDtypeStruct((b, s, MIXOUT_WIDTH), BF16),
        grid=(b // nsq, s // t),
        in_specs=[
            pl.BlockSpec((nsq, t, PROJ_COLS), lambda i, j: (i, j, 0)),
            whole((CHUNK, CHUNK)),
            _layer_spec(layer, (MLSTM_CONV, N_HEADS * MLSTM_DH)),
            _layer_spec(layer, (1, N_HEADS * MLSTM_DH)),
            pl.BlockSpec((3, None, N_HEADS, MLSTM_DH, MLSTM_DH), lambda i, j: (0, layer, 0, 0, 0)),
        ],
        out_specs=pl.BlockSpec((nsq, t, MIXOUT_WIDTH), lambda i, j: (i, j, 0)),
        scratch_shapes=[
            pltpu.VMEM((nsq, N_HEADS, HGRN_D, HGRN_D), F32),
            pltpu.VMEM((nsq, N_HEADS // 2, GLA_DV, LANES), F32),
            pltpu.VMEM((nsq, N_HEADS, MLSTM_DH, MLSTM_DH), F32),
            pltpu.VMEM((nsq, SUB, MLSTM_DH), F32),
            pltpu.VMEM((nsq, SUB, LANES), F32),
            pltpu.VMEM((nsq, SUB + CHUNK, N_HEADS * MLSTM_DH), F32),
        ],
        compiler_params=pltpu.CompilerParams(
            dimension_semantics=("arbitrary", "arbitrary"), vmem_limit_bytes=VMEM_LIMIT),
        name="mixers",
    )(proj, tri, convw, convb, wqkv)


def _outattn_kernel(mix_ref, az_ref, gz_ref, mz_ref, x_ref, hnorm_ref, gnorm_ref, mnorm_ref, skip_ref,
                    wout_ref, gx_ref, wq_ref, k_ref, v_ref, wo_ref, gf_ref, o_ref, *, final_norm):
    tm = x_ref.shape[0]
    halves = [slice(a * (tm // 4), (a + 1) * (tm // 4)) for a in range(4)] if tm % 64 == 0 else [slice(0, tm)]

    def mixed(r):
        parts = []
        for c0, gate_ref, g_ref in ((0, az_ref, hnorm_ref), (N_HEADS * HGRN_D, gz_ref, gnorm_ref)):
            for h in range(N_HEADS):
                oh = mix_ref[r, c0 + h * 128:c0 + (h + 1) * 128].astype(F32)
                y = oh * lax.rsqrt(jnp.mean(oh * oh, axis=-1, keepdims=True) + RMS_EPS) * g_ref[...]
                parts.append((y * gate_ref[r, h * 128:(h + 1) * 128]).astype(BF16))
        for h in range(N_HEADS):
            sl = slice(h * MLSTM_DH, (h + 1) * MLSTM_DH)
            hh = mix_ref[r, O_MLSTM + h * MLSTM_DH:O_MLSTM + (h + 1) * MLSTM_DH].astype(F32)
            conv = mix_ref[r, O_CONV + h * MLSTM_DH:O_CONV + (h + 1) * MLSTM_DH].astype(F32)
            hc = hh - jnp.mean(hh, axis=-1, keepdims=True)
            y = hc * lax.rsqrt(jnp.mean(hc * hc, axis=-1, keepdims=True) + RMS_EPS) * mnorm_ref[:, sl]
            parts.append(((y + skip_ref[:, sl] * conv) * mz_ref[r, sl]).astype(BF16))
        return jnp.concatenate(parts, axis=-1)

    x1 = [x_ref[r, :] + _dot(mixed(r), wout_ref[...]) for r in halves]
    q = [_dot(_rms_norm(t, gx_ref[...]).astype(BF16), wq_ref[...]).astype(BF16) for t in x1]
    items = [(a, hd) for a in range(len(halves)) for hd in range(N_HEADS)]

    def scores(item):
        a, hd = item
        sl = slice(hd * XA_HEAD_DIM, (hd + 1) * XA_HEAD_DIM)
        return _dot_nt(q[a][:, sl], k_ref[:, sl]) * (XA_HEAD_DIM ** -0.5)

    lookahead = 2
    pending = [scores(it) for it in items[:lookahead]]
    outs = [[] for _ in halves]
    for n, (a, hd) in enumerate(items):
        s = pending.pop(0)
        e = jnp.exp(s - jnp.max(s, axis=-1, keepdims=True))
        p = e / jnp.sum(e, axis=-1, keepdims=True)
        sl = slice(hd * XA_HEAD_DIM, (hd + 1) * XA_HEAD_DIM)
        outs[a].append(_dot(p.astype(BF16), v_ref[:, sl]).astype(BF16))
        if n + lookahead < len(items):
            pending.append(scores(items[n + lookahead]))
    for a, r in enumerate(halves):
        x2 = x1[a] + _dot(jnp.concatenate(outs[a], axis=-1), wo_ref[...])
        if final_norm:
            x2 = _rms_norm(x2, gf_ref[...])
        o_ref[r, :] = x2


def _outattn(mix, proj, x, hnorm, gnorm, mnorm, skip, wout, gx, wq, kmem, vmem, wo, gf, layer, tm,
             final_norm):
    b, s, _ = x.shape
    m = kmem.shape[1]
    gate_w = N_HEADS * 128

    return pl.pallas_call(
        functools.partial(_outattn_kernel, final_norm=final_norm),
        out_shape=jax.ShapeDtypeStruct((b, s, D_MODEL), F32),
        grid=(b, s // tm),
        in_specs=[
            pl.BlockSpec((None, tm, MIXOUT_WIDTH), lambda i, j: (i, j, 0)),
            pl.BlockSpec((None, tm, gate_w), lambda i, j: (i, j, C_AZ // gate_w)),
            pl.BlockSpec((None, tm, gate_w), lambda i, j: (i, j, C_GZ // gate_w)),
            pl.BlockSpec((None, tm, gate_w), lambda i, j: (i, j, C_MZ // gate_w)),
            pl.BlockSpec((None, tm, D_MODEL), lambda i, j: (i, j, 0)),
            _layer_spec(layer, (1, HGRN_D)),
            _layer_spec(layer, (1, GLA_DV)),
            _layer_spec(layer, (1, N_HEADS * MLSTM_DH)),
            _layer_spec(layer, (1, N_HEADS * MLSTM_DH)),
            _layer_spec(layer, (MIX_WIDTH, D_MODEL)),
            _layer_spec(layer, (1, D_MODEL)),
            _layer_spec(layer, (D_MODEL, D_MODEL)),
            pl.BlockSpec((None, m, D_MODEL), lambda i, j: (i, 0, 0)),
            pl.BlockSpec((None, m, D_MODEL), lambda i, j: (i, 0, 0)),
            _layer_spec(layer, (D_MODEL, D_MODEL)),
            pl.BlockSpec((1, D_MODEL), lambda i, j: (0, 0)),
        ],
        out_specs=pl.BlockSpec((None, tm, D_MODEL), lambda i, j: (i, j, 0)),
        compiler_params=pltpu.CompilerParams(
            dimension_semantics=("arbitrary", "arbitrary"), vmem_limit_bytes=OUTATTN_VMEM_LIMIT),
        name="outattn",
    )(mix, proj, proj, proj, x, hnorm, gnorm, mnorm, skip, wout, gx, wq, kmem, vmem, wo, gf)


def _pack_w_in(w):
    ga0 = 2048 + 2 * N_HEADS * GLA_DK + N_HEADS * GLA_DV
    ga1 = ga0 + GLA_RANK
    gate0 = ga1 + N_HEADS * GLA_DV + 2 * N_HEADS * MLSTM_DH
    assert ga0 == C_GZ
    pad = jnp.zeros(w.shape[:-1] + (LANES - GLA_RANK - 2 * N_HEADS,), w.dtype)
    tail = jnp.concatenate([w[..., ga1:gate0], w[..., ga0:ga1], w[..., gate0:], pad], axis=-1)
    return w[..., :ga0].astype(BF16), tail.astype(BF16)


def _block_diag_dense(w):
    lead = w.shape[:-3]
    blk = w.shape[-1]
    rows = w.reshape(lead + (N_HEADS, MLSTM_DH, blk))
    tiled = jnp.tile(rows, (1,) * (len(lead) + 2) + (MLSTM_DH // blk,))
    r = np.arange(MLSTM_DH)
    on_diag = jnp.asarray((r[:, None] // blk) == (r[None, :] // blk))
    return jnp.where(on_diag, tiled, 0.0).astype(BF16)


def kernel(x, mem, norm_mix, w_in, hgrn_lb_logits, hgrn_norm, gla_gate_up, gla_gate_bias, gla_norm,
           mlstm_conv_w, mlstm_conv_b, mlstm_wq, mlstm_wk, mlstm_wv, mlstm_igate_bias,
           mlstm_fgate_bias, mlstm_skip, mlstm_norm, w_out, norm_xattn, norm_mem, xa_wq, xa_wk,
           xa_wv, xa_wo, norm_final):
    b, s, d = x.shape
    m = mem.shape[1]
    depth = w_in.shape[0]
    tri = jnp.asarray(np.tril(np.ones((CHUNK, CHUNK), np.float32)), dtype=BF16)
    mem2d = mem.reshape(b * m, d)

    def row(p):
        return p[:, None, :]

    w_in_a, w_in_b = _pack_w_in(w_in)
    gup = jnp.concatenate(
        [gla_gate_up, jnp.zeros((depth, LANES - GLA_RANK, N_HEADS * GLA_DK), F32)], axis=1).astype(BF16)
    sbias = row(jnp.concatenate(
        [jnp.zeros((depth, GLA_RANK), F32), mlstm_igate_bias, mlstm_fgate_bias,
         jnp.zeros((depth, LANES - GLA_RANK - 2 * N_HEADS), F32)], axis=1))
    wqkv = _block_diag_dense(jnp.stack([mlstm_wq, mlstm_wk, mlstm_wv]))
    w_out_b, xa_wq_b, xa_wk_b, xa_wv_b, xa_wo_b = (
        t.astype(BF16) for t in (w_out, xa_wq, xa_wk, xa_wv, xa_wo))

    for l in range(depth):
        proj = _inproj(x.reshape(b * s, d), row(norm_mix), w_in_a, w_in_b, hgrn_lb_logits, gup,
                       row(gla_gate_bias), sbias, layer=l,
                       tm=min(512, b * s)).reshape(b, s, PROJ_COLS)
        mix = _mixers(proj, tri, mlstm_conv_w, row(mlstm_conv_b), wqkv, layer=l)
        kmem, vmem = _memkv(mem2d, row(norm_mem), xa_wk_b, xa_wv_b, layer=l, tm=256)
        x = _outattn(mix, proj, x, row(hgrn_norm), row(gla_norm), row(mlstm_norm), row(mlstm_skip),
                     w_out_b, row(norm_xattn), xa_wq_b, kmem.reshape(b, m, d), vmem.reshape(b, m, d),
                     xa_wo_b, norm_final[None, :], layer=l, tm=min(1024, s),
                     final_norm=(l == depth - 1))
    return x
```

```python
import functools
import math

import numpy as np
import jax
import jax.numpy as jnp
from jax import lax
from jax.experimental import pallas as pl
from jax.experimental.pallas import tpu as pltpu

F32 = jnp.float32
BF16 = jnp.bfloat16

RMS_EPS = 1e-6
D_MODEL = 1024
CHUNK = 64
STEP_CHUNKS = 4
STEP_SEQS = 2
SUB = 8
LANES = 128
N_HEADS = 4
HGRN_D = 128
GLA_DK = 64
GLA_DV = 128
GLA_RANK = 16
GLA_GATE_NORMALIZER = 16.0
MLSTM_DH = 128
MLSTM_CONV = 4
XA_HEAD_DIM = D_MODEL // N_HEADS
MIX_WIDTH = 3 * N_HEADS * 128
O_MLSTM = 2 * N_HEADS * 128
O_CONV = MIX_WIDTH
MIXOUT_WIDTH = MIX_WIDTH + N_HEADS * MLSTM_DH

C_AQ, C_AF, C_AI, C_AZ = 0, 512, 1024, 1536
C_GQ, C_GK, C_GV, C_GZ = 2048, 2304, 2560, 3072
C_MU, C_MZ = 3584, 4096
C_SMALL = 4608
W_COLS = C_SMALL + LANES
C_HK = W_COLS
C_GLF = C_HK + 512
PROJ_COLS = C_GLF + 256
L_MI = GLA_RANK
L_MF = GLA_RANK + N_HEADS

N_LEVELS = 6
LOG_ZERO = -1e30
LOG2E = math.log2(math.e)
VMEM_LIMIT = 48 * 1024 * 1024
OUTATTN_VMEM_LIMIT = 56 * 1024 * 1024


def _exp_neg(x):
    return jnp.exp2(x * (-LOG2E))


def _sigmoid(x):
    return 1.0 / (1.0 + _exp_neg(x))


def _silu(x):
    return x * _sigmoid(x)


def _log_sigmoid(x):
    return jnp.minimum(x, 0.0) - jnp.log(1.0 + _exp_neg(jnp.abs(x)))


def _dot(a, b):
    return jnp.dot(a, b, preferred_element_type=F32)


def _dot_nt(a, b):
    return lax.dot_general(a, b, (((1,), (1,)), ((), ())), preferred_element_type=F32)


def _dot_tn(a, b):
    return lax.dot_general(a, b, (((0,), (0,)), ((), ())), preferred_element_type=F32)


def _rms_norm(x, g):
    return x * lax.rsqrt(jnp.mean(x * x, axis=-1, keepdims=True) + RMS_EPS) * g


def _split3(x):
    hi = x.astype(BF16)
    r1 = x - hi.astype(F32)
    mid = r1.astype(BF16)
    lo = (r1 - mid.astype(F32)).astype(BF16)
    return hi, mid, lo


def _cumsum_rows(tri_bf16, x):
    hi = x.astype(BF16)
    lo = (x - hi.astype(F32)).astype(BF16)
    return _dot(tri_bf16, hi) + _dot(tri_bf16, lo)


def _layer_spec(layer, shape):
    return pl.BlockSpec((None,) + tuple(shape), lambda *_: (layer,) + (0,) * len(shape))


def _hgrn_lower_bound(lb_logit_ref, layer):
    logits = lb_logit_ref[...]
    ex = jnp.exp(logits - jnp.max(logits, axis=0, keepdims=True))
    soft = ex / jnp.sum(ex, axis=0, keepdims=True)
    c0 = soft[0:1, :]
    cl = c0
    for r in range(1, layer + 1):
        cl = cl + soft[r:r + 1, :]
    return cl - c0


def _inproj_kernel(x_ref, g_ref, wa_ref, wb_ref, lb_logit_ref, gup_ref, gbias_ref, sbias_ref, o_ref,
                   *, layer):
    tm = x_ref.shape[0]
    h = _rms_norm(x_ref[...], g_ref[...]).astype(BF16)

    def cols(c0, c1):
        if c1 <= C_GZ:
            return _dot(h, wa_ref[:, c0:c1])
        return _dot(h, wb_ref[:, c0 - C_GZ:c1 - C_GZ])

    lb = _hgrn_lower_bound(lb_logit_ref, layer)
    one_m_lb = 1.0 - lb
    o_ref[:, C_AQ:C_AF] = _silu(cols(C_AQ, C_AF))
    sig_f = _sigmoid(cols(C_AF, C_AI))
    o_ref[:, C_AF:C_AI] = jnp.maximum(jnp.log(lb + one_m_lb * sig_f) * LOG2E, LOG_ZERO)
    o_ref[:, C_HK:C_GLF] = one_m_lb * (1.0 - sig_f)
    o_ref[:, C_AI:C_AZ] = cols(C_AI, C_AZ)
    o_ref[:, C_AZ:C_GQ] = _silu(cols(C_AZ, C_GQ))
    o_ref[:, C_GQ:C_GK] = cols(C_GQ, C_GK) * (GLA_DK ** -0.5)
    o_ref[:, C_GK:C_GZ] = cols(C_GK, C_GZ)
    o_ref[:, C_GZ:C_MU] = _silu(cols(C_GZ, C_MU))
    o_ref[:, C_MU:C_MZ] = cols(C_MU, C_MZ)
    o_ref[:, C_MZ:C_SMALL] = _silu(cols(C_MZ, C_SMALL))
    small = cols(C_SMALL, W_COLS)
    gla_pre = _dot(small.astype(BF16), gup_ref[...]) + gbias_ref[...]
    o_ref[:, C_GLF:PROJ_COLS] = _log_sigmoid(gla_pre) * (LOG2E / GLA_GATE_NORMALIZER)
    lane = lax.broadcasted_iota(jnp.int32, (tm, LANES), 1)
    sb = small + sbias_ref[...]
    is_f = jnp.logical_and(lane >= L_MF, lane < L_MF + N_HEADS)
    is_i = jnp.logical_and(lane >= L_MI, lane < L_MI + N_HEADS)
    o_ref[:, C_SMALL:W_COLS] = jnp.where(is_i, sb, jnp.where(is_f, _log_sigmoid(sb), 0.0))


def _inproj(x2d, g, wa, wb, lb_logits, gup, gbias, sbias, layer, tm):
    n = x2d.shape[0]
    depth = lb_logits.shape[0]
    return pl.pallas_call(
        functools.partial(_inproj_kernel, layer=layer),
        out_shape=jax.ShapeDtypeStruct((n, PROJ_COLS), F32),
        grid=(n // tm,),
        in_specs=[
            pl.BlockSpec((tm, D_MODEL), lambda i: (i, 0)),
            _layer_spec(layer, (1, D_MODEL)),
            _layer_spec(layer, (D_MODEL, C_GZ)),
            _layer_spec(layer, (D_MODEL, W_COLS - C_GZ)),
            pl.BlockSpec((depth, N_HEADS * HGRN_D), lambda i: (0, 0)),
            _layer_spec(layer, (LANES, N_HEADS * GLA_DK)),
            _layer_spec(layer, (1, N_HEADS * GLA_DK)),
            _layer_spec(layer, (1, LANES)),
        ],
        out_specs=pl.BlockSpec((tm, PROJ_COLS), lambda i: (i, 0)),
        compiler_params=pltpu.CompilerParams(
            dimension_semantics=("arbitrary",), vmem_limit_bytes=VMEM_LIMIT),
        name="inproj",
    )(x2d, g, wa, wb, lb_logits, gup, gbias, sbias)


def _memkv_kernel(m_ref, g_ref, wk_ref, wv_ref, k_ref, v_ref):
    h = _rms_norm(m_ref[...], g_ref[...]).astype(BF16)
    k_ref[...] = _dot(h, wk_ref[...]).astype(BF16)
    v_ref[...] = _dot(h, wv_ref[...]).astype(BF16)


def _memkv(mem2d, g, wk, wv, layer, tm):
    n = mem2d.shape[0]
    full = _layer_spec(layer, (D_MODEL, D_MODEL))
    tile = pl.BlockSpec((tm, D_MODEL), lambda i: (i, 0))
    return pl.pallas_call(
        _memkv_kernel,
        out_shape=(jax.ShapeDtypeStruct((n, D_MODEL), BF16),) * 2,
        grid=(n // tm,),
        in_specs=[tile, _layer_spec(layer, (1, D_MODEL)), full, full],
        out_specs=(tile, tile),
        compiler_params=pltpu.CompilerParams(
            dimension_semantics=("arbitrary",), vmem_limit_bytes=VMEM_LIMIT),
        name="memkv",
    )(mem2d, g, wk, wv)


def _row_signs(w):
    row = lax.broadcasted_iota(jnp.int32, (CHUNK // SUB, SUB, w), 1)
    sgn4 = jnp.where(row >= 4, 1.0, -1.0)
    sgn2 = jnp.where((row & 3) >= 2, 1.0, -1.0)
    odd = jnp.where((row & 1) == 1, 1.0, 0.0)
    return row < 4, sgn4, sgn2, odd


def _decay_exponents(b, lf, signs):
    w = b.shape[-1]
    groups = CHUNK // SUB
    first4, sgn4, sgn2, odd = (t[:, :, :w] for t in signs)
    b3 = b.reshape(groups, SUB, w)
    levels = []
    for m in (32, 16, 8):
        gm = m // SUB
        parts = []
        for blk in range(groups // (2 * gm)):
            g0 = blk * 2 * gm
            ref = b3[g0 + gm - 1:g0 + gm, SUB - 1:SUB, :]
            parts.append(ref - b3[g0:g0 + gm])
            parts.append(b3[g0 + gm:g0 + 2 * gm] - ref)
        levels.append(jnp.concatenate(parts, axis=0))
    levels.append((b3 - b3[:, 3:4, :]) * sgn4)
    levels.append((b3 - jnp.where(first4, b3[:, 1:2, :], b3[:, 5:6, :])) * sgn2)
    levels.append(lf.reshape(groups, SUB, w) * odd)
    e_kstate = b3[groups - 1:groups, SUB - 1:SUB, :] - b3
    return [e.reshape(CHUNK, w) for e in levels], e_kstate.reshape(CHUNK, w)


def _pair_masks():
    i = lax.broadcasted_iota(jnp.int32, (CHUNK, LANES), 0)
    j = lax.broadcasted_iota(jnp.int32, (CHUNK, LANES), 1) & (CHUNK - 1)
    x = jnp.bitwise_xor(i, j)
    below = i > j
    masks = [jnp.where(i == j, 1.0, 0.0)]
    for lvl in range(N_LEVELS):
        top_bit = N_LEVELS - 1 - lvl
        masks.append(jnp.where(jnp.logical_and(below, jnp.right_shift(x, top_bit) == 1), 1.0, 0.0))
    return masks


def _block_rows(t0, t1):
    z = jnp.zeros_like(t0)
    return jnp.concatenate([jnp.concatenate([t0, z], axis=1), jnp.concatenate([z, t1], axis=1)], axis=0)


def _pair_scores(q16, levels, masks, key_blocks):
    scores = _dot_nt(q16, key_blocks(None)) * masks[0]
    for lvl in range(N_LEVELS):
        ee = jnp.exp2(levels[lvl]).astype(BF16)
        scores = scores + _dot_nt(q16 * ee, key_blocks(ee)) * masks[1 + lvl]
    return scores


def _lockstep(gens):
    live = list(gens)
    while live:
        live = [g for g in live if next(g, StopIteration) is not StopIteration]


def _mixer_kernel(proj_ref, tri_ref, convw_ref, convb_ref, wqkv_ref,
                  out_ref,
                  hst_ref, gst_ref, mc_ref, mn_ref, mm_ref, conv_ref):
    nsq = proj_ref.shape[0]

    @pl.when(pl.program_id(1) == 0)
    def _():
        hst_ref[...] = jnp.zeros_like(hst_ref)
        gst_ref[...] = jnp.zeros_like(gst_ref)
        mc_ref[...] = jnp.zeros_like(mc_ref)
        mn_ref[...] = jnp.zeros_like(mn_ref)
        mm_ref[...] = jnp.zeros_like(mm_ref)
        conv_ref[:, 0:SUB, :] = jnp.zeros((nsq, SUB, N_HEADS * MLSTM_DH), F32)

    tri = tri_ref[...]
    masks = _pair_masks()
    signs = _row_signs(2 * HGRN_D)
    lane = lax.broadcasted_iota(jnp.int32, (CHUNK, LANES), 1)
    low_half = lane < GLA_DK
    is_i = jnp.logical_and(lane >= L_MI, lane < L_MI + N_HEADS)
    lane_st = lax.broadcasted_iota(jnp.int32, (GLA_DV, LANES), 1)
    eye = (lax.broadcasted_iota(jnp.int32, (LANES, LANES), 0)
           == lax.broadcasted_iota(jnp.int32, (LANES, LANES), 1)).astype(BF16)
    ci = lax.broadcasted_iota(jnp.int32, (CHUNK, CHUNK), 0)
    cj = lax.broadcasted_iota(jnp.int32, (CHUNK, CHUNK), 1)
    causal = ci >= cj

    def chunk(sq, r0):
        rows = slice(r0, r0 + CHUNK)

        small = proj_ref[sq, rows, C_SMALL:C_SMALL + LANES]
        h_lf = proj_ref[sq, rows, C_AF:C_AF + N_HEADS * HGRN_D]
        h_k = proj_ref[sq, rows, C_HK:C_HK + N_HEADS * HGRN_D]
        m_lf = jnp.where(is_i, 0.0, small)

        b_hm = _cumsum_rows(tri, jnp.concatenate([h_lf, m_lf], axis=1))
        h_b = b_hm[:, :N_HEADS * HGRN_D]
        m_b = b_hm[:, N_HEADS * HGRN_D:]
        yield

        m_u = proj_ref[sq, rows, C_MU:C_MU + N_HEADS * MLSTM_DH]
        conv_ref[sq, SUB:SUB + CHUNK, :] = m_u
        acc = convb_ref[...]
        for tap in range(MLSTM_CONV):
            off = SUB - (MLSTM_CONV - 1) + tap
            acc = acc + conv_ref[sq, off:off + CHUNK, :] * convw_ref[tap:tap + 1, :]
        conv_ref[sq, 0:SUB, :] = m_u[CHUNK - SUB:CHUNK, :]
        conv = _silu(acc)
        m_q, m_k, m_v = [], [], []
        for h in range(N_HEADS):
            sl = slice(h * MLSTM_DH, (h + 1) * MLSTM_DH)
            cb = conv[:, sl].astype(BF16)
            m_q.append(_dot(cb, wqkv_ref[0, h]))
            m_k.append(_dot(cb, wqkv_ref[1, h]) * (MLSTM_DH ** -0.5))
            m_v.append(_dot(m_u[:, sl].astype(BF16), wqkv_ref[2, h]).astype(BF16))
        yield

        z = jnp.where(is_i, small, m_b)
        zh = z.astype(BF16)
        zl = (z - zh.astype(F32)).astype(BF16)
        z_t = _dot_nt(eye, zh) + _dot_nt(eye, zl)

        g_lf = proj_ref[sq, rows, C_GLF:C_GLF + N_HEADS * GLA_DK]
        g_b = _cumsum_rows(tri, g_lf)
        yield

        h_q = proj_ref[sq, rows, C_AQ:C_AQ + N_HEADS * HGRN_D]
        h_scores, h_vb = [], []
        for pair in range(N_HEADS // 2):
            psl = slice(pair * 2 * HGRN_D, (pair + 1) * 2 * HGRN_D)
            q, k, b = h_q[:, psl], h_k[:, psl], h_b[:, psl]
            levels, e_k = _decay_exponents(b, h_lf[:, psl], signs)
            q_inter = (q * jnp.exp2(b)).astype(BF16)
            k_state = (k * jnp.exp2(e_k)).astype(BF16)
            decay = jnp.exp2(b[CHUNK - 1:CHUNK, :])
            vb = proj_ref[sq, rows,
                          C_AI + pair * 2 * HGRN_D:C_AI + (pair + 1) * 2 * HGRN_D].astype(BF16)
            o_inter = []
            for hp in range(2):
                h = 2 * pair + hp
                hsl = slice(hp * HGRN_D, (hp + 1) * HGRN_D)
                st = hst_ref[sq, h]
                o_inter.append(_dot_nt(q_inter[:, hsl], st.astype(BF16)))
                hst_ref[sq, h] = decay[:, hsl] * st + _dot_tn(vb[:, hsl], k_state[:, hsl])
            k16 = k.astype(BF16)

            def h_keys(scale, k16=k16):
                t = k16 if scale is None else k16 * scale
                return _block_rows(t[:, :HGRN_D], t[:, HGRN_D:])

            h_scores.append((jnp.concatenate(o_inter, axis=1),
                             _pair_scores(q.astype(BF16), levels, masks, h_keys)))
            h_vb.append(vb)
            yield

        m_parts = []
        for h in range(N_HEADS):
            q, k, vb = m_q[h], m_k[h], m_v[h]
            qb, kb = q.astype(BF16), k.astype(BF16)
            bc = z[:, L_MF + h:L_MF + h + 1]
            ic = z[:, L_MI + h:L_MI + h + 1]
            br = z_t[L_MF + h:L_MF + h + 1, :]
            ir = z_t[L_MI + h:L_MI + h + 1, :]
            m_prev = mm_ref[sq, h:h + 1, 0:1]
            c_prev = mc_ref[sq, h]
            n_prev = mn_ref[sq, h:h + 1, :]

            log_d = jnp.where(causal, bc - br + ir, -jnp.inf)
            m_inter = bc + m_prev
            m_i = jnp.maximum(m_inter, jnp.max(log_d, axis=-1, keepdims=True))
            w_inter = jnp.exp(m_inter - m_i)
            s = _dot_nt(qb, kb) * jnp.exp(log_d - m_i)
            num_inter = w_inter * _dot(qb, c_prev.astype(BF16))
            den = (w_inter * jnp.sum(q * n_prev, axis=-1, keepdims=True)
                   + jnp.sum(s, axis=-1, keepdims=True))
            scale = 1.0 / jnp.maximum(jnp.abs(den), jnp.exp(-m_i))

            b_last = bc[CHUNK - 1:CHUNK, :]
            log_w = b_last - bc + ic
            m_new = jnp.maximum(b_last + m_prev, jnp.max(log_w, axis=0, keepdims=True))
            w_prev = jnp.exp(b_last + m_prev - m_new)
            kw = jnp.exp(log_w - m_new) * k
            mc_ref[sq, h] = w_prev * c_prev + _dot_tn(kw.astype(BF16), vb)
            mn_ref[sq, h:h + 1, :] = w_prev * n_prev + jnp.sum(kw, axis=0, keepdims=True)
            mm_ref[sq, h:h + 1, :] = jnp.broadcast_to(m_new, (1, LANES))
            m_parts.append((num_inter, s.astype(BF16), scale))
        yield

        g_scores, g_vb = [], []
        for pair in range(N_HEADS // 2):
            psl = slice(pair * LANES, (pair + 1) * LANES)
            q = proj_ref[sq, rows, C_GQ + pair * LANES:C_GQ + (pair + 1) * LANES]
            k = proj_ref[sq, rows, C_GK + pair * LANES:C_GK + (pair + 1) * LANES]
            b = g_b[:, psl]
            levels, e_k = _decay_exponents(b, g_lf[:, psl], signs)
            q_inter = q * jnp.exp2(b)
            k_state = (k * jnp.exp2(e_k)).astype(BF16)
            decay = jnp.exp2(b[CHUNK - 1:CHUNK, :])
            vb = proj_ref[sq, rows,
                          C_GV + pair * 2 * GLA_DV:C_GV + (pair + 1) * 2 * GLA_DV].astype(BF16)
            st = gst_ref[sq, pair]
            stb = st.astype(BF16)
            o_inter = [_dot_nt(jnp.where(low_half, q_inter, 0.0).astype(BF16), stb),
                       _dot_nt(jnp.where(low_half, 0.0, q_inter).astype(BF16), stb)]
            gst_ref[sq, pair] = decay * st + jnp.where(lane_st < GLA_DK,
                                                       _dot_tn(vb[:, :GLA_DV], k_state),
                                                       _dot_tn(vb[:, GLA_DV:], k_state))
            k_lo = jnp.where(low_half, k, 0.0).astype(BF16)
            k_hi = jnp.where(low_half, 0.0, k).astype(BF16)

            def g_keys(scale, k_lo=k_lo, k_hi=k_hi):
                if scale is None:
                    return jnp.concatenate([k_lo, k_hi], axis=0)
                return jnp.concatenate([k_lo * scale, k_hi * scale], axis=0)

            g_scores.append((jnp.concatenate(o_inter, axis=1),
                             _pair_scores(q.astype(BF16), levels, masks, g_keys)))
            g_vb.append(vb)
            yield

        for pair in range(N_HEADS // 2):
            o_inter, scores = h_scores[pair]
            vb = h_vb[pair]
            o = o_inter + _dot(scores.astype(BF16), _block_rows(vb[:, :HGRN_D], vb[:, HGRN_D:]))
            out_ref[sq, rows, pair * 2 * HGRN_D:(pair + 1) * 2 * HGRN_D] = o.astype(out_ref.dtype)
        yield

        for h in range(N_HEADS):
            num_inter, sbf, scale = m_parts[h]
            hh = (num_inter + _dot(sbf, m_v[h])) * scale
            out_ref[sq, rows, O_MLSTM + h * 128:O_MLSTM + (h + 1) * 128] = hh.astype(out_ref.dtype)
        out_ref[sq, rows, O_CONV:MIXOUT_WIDTH] = conv.astype(out_ref.dtype)
        yield

        for pair in range(N_HEADS // 2):
            o_inter, scores = g_scores[pair]
            vb = g_vb[pair]
            o = o_inter + _dot(scores.astype(BF16), _block_rows(vb[:, :GLA_DV], vb[:, GLA_DV:]))
            out_ref[sq, rows, 512 + pair * 2 * GLA_DV:512 + (pair + 1) * 2 * GLA_DV] = o.astype(
                out_ref.dtype)

    for c in range(proj_ref.shape[1] // CHUNK):
        _lockstep([chunk(sq, c * CHUNK) for sq in range(nsq)])


def _mixers(proj, tri, convw, convb, wqkv, layer):
    b, s, _ = proj.shape
    t = min(CHUNK * STEP_CHUNKS, s)
    nsq = STEP_SEQS if b % STEP_SEQS == 0 else 1

    def whole(shape):
        return pl.BlockSpec(shape, lambda i, j: (0,) * len(shape))

    return pl.pallas_call(
        _mixer_kernel,
        out_shape=jax.ShapeDtypeStruct((b, s, MIXOUT_WIDTH), BF16),
        grid=(b // nsq, s // t),
        in_specs=[
            pl.BlockSpec((nsq, t, PROJ_COLS), lambda i, j: (i, j, 0)),
            whole((CHUNK, CHUNK)),
            _layer_spec(layer, (MLSTM_CONV, N_HEADS * MLSTM_DH)),
            _layer_spec(layer, (1, N_HEADS * MLSTM_DH)),
            pl.BlockSpec((3, None, N_HEADS, MLSTM_DH, MLSTM_DH), lambda i, j: (0, layer, 0, 0, 0)),
        ],
        out_specs=pl.BlockSpec((nsq, t, MIXOUT_WIDTH), lambda i, j: (i, j, 0)),
        scratch_shapes=[
            pltpu.VMEM((nsq, N_HEADS, HGRN_D, HGRN_D), F32),
            pltpu.VMEM((nsq, N_HEADS // 2, GLA_DV, LANES), F32),
            pltpu.VMEM((nsq, N_HEADS, MLSTM_DH, MLSTM_DH), F32),
            pltpu.VMEM((nsq, SUB, MLSTM_DH), F32),
            pltpu.VMEM((nsq, SUB, LANES), F32),
            pltpu.VMEM((nsq, SUB + CHUNK, N_HEADS * MLSTM_DH), F32),
        ],
        compiler_params=pltpu.CompilerParams(
            dimension_semantics=("arbitrary", "arbitrary"), vmem_limit_bytes=VMEM_LIMIT),
        name="mixers",
    )(proj, tri, convw, convb, wqkv)


def _outattn_kernel(mix_ref, az_ref, gz_ref, mz_ref, x_ref, hnorm_ref, gnorm_ref, mnorm_ref, skip_ref,
                    wout_ref, gx_ref, wq_ref, k_ref, v_ref, wo_ref, gf_ref, o_ref, *, final_norm):
    tm = x_ref.shape[0]
    halves = [slice(a * (tm // 4), (a + 1) * (tm // 4)) for a in range(4)] if tm % 64 == 0 else [slice(0, tm)]

    def mixed(r):
        parts = []
        for c0, gate_ref, g_ref in ((0, az_ref, hnorm_ref), (N_HEADS * HGRN_D, gz_ref, gnorm_ref)):
            for h in range(N_HEADS):
                oh = mix_ref[r, c0 + h * 128:c0 + (h + 1) * 128].astype(F32)
                y = oh * lax.rsqrt(jnp.mean(oh * oh, axis=-1, keepdims=True) + RMS_EPS) * g_ref[...]
                parts.append((y * gate_ref[r, h * 128:(h + 1) * 128]).astype(BF16))
        for h in range(N_HEADS):
            sl = slice(h * MLSTM_DH, (h + 1) * MLSTM_DH)
            hh = mix_ref[r, O_MLSTM + h * MLSTM_DH:O_MLSTM + (h + 1) * MLSTM_DH].astype(F32)
            conv = mix_ref[r, O_CONV + h * MLSTM_DH:O_CONV + (h + 1) * MLSTM_DH].astype(F32)
            hc = hh - jnp.mean(hh, axis=-1, keepdims=True)
            y = hc * lax.rsqrt(jnp.mean(hc * hc, axis=-1, keepdims=True) + RMS_EPS) * mnorm_ref[:, sl]
            parts.append(((y + skip_ref[:, sl] * conv) * mz_ref[r, sl]).astype(BF16))
        return jnp.concatenate(parts, axis=-1)

    x1 = [x_ref[r, :] + _dot(mixed(r), wout_ref[...]) for r in halves]
    q = [_dot(_rms_norm(t, gx_ref[...]).astype(BF16), wq_ref[...]).astype(BF16) for t in x1]
    items = [(a, hd) for a in range(len(halves)) for hd in range(N_HEADS)]

    def scores(item):
        a, hd = item
        sl = slice(hd * XA_HEAD_DIM, (hd + 1) * XA_HEAD_DIM)
        return _dot_nt(q[a][:, sl], k_ref[:, sl]) * (XA_HEAD_DIM ** -0.5)

    lookahead = 2
    pending = [scores(it) for it in items[:lookahead]]
    outs = [[] for _ in halves]
    for n, (a, hd) in enumerate(items):
        s = pending.pop(0)
        e = jnp.exp(s - jnp.max(s, axis=-1, keepdims=True))
        p = e / jnp.sum(e, axis=-1, keepdims=True)
        sl = slice(hd * XA_HEAD_DIM, (hd + 1) * XA_HEAD_DIM)
        outs[a].append(_dot(p.astype(BF16), v_ref[:, sl]).astype(BF16))
        if n + lookahead < len(items):
            pending.append(scores(items[n + lookahead]))
    for a, r in enumerate(halves):
        x2 = x1[a] + _dot(jnp.concatenate(outs[a], axis=-1), wo_ref[...])
        if final_norm:
            x2 = _rms_norm(x2, gf_ref[...])
        o_ref[r, :] = x2


def _outattn(mix, proj, x, hnorm, gnorm, mnorm, skip, wout, gx, wq, kmem, vmem, wo, gf, layer, tm,
             final_norm):
    b, s, _ = x.shape
    m = kmem.shape[1]
    gate_w = N_HEADS * 128

    return pl.pallas_call(
        functools.partial(_outattn_kernel, final_norm=final_norm),
        out_shape=jax.ShapeDtypeStruct((b, s, D_MODEL), F32),
        grid=(b, s // tm),
        in_specs=[
            pl.BlockSpec((None, tm, MIXOUT_WIDTH), lambda i, j: (i, j, 0)),
            pl.BlockSpec((None, tm, gate_w), lambda i, j: (i, j, C_AZ // gate_w)),
            pl.BlockSpec((None, tm, gate_w), lambda i, j: (i, j, C_GZ // gate_w)),
            pl.BlockSpec((None, tm, gate_w), lambda i, j: (i, j, C_MZ // gate_w)),
            pl.BlockSpec((None, tm, D_MODEL), lambda i, j: (i, j, 0)),
            _layer_spec(layer, (1, HGRN_D)),
            _layer_spec(layer, (1, GLA_DV)),
            _layer_spec(layer, (1, N_HEADS * MLSTM_DH)),
            _layer_spec(layer, (1, N_HEADS * MLSTM_DH)),
            _layer_spec(layer, (MIX_WIDTH, D_MODEL)),
            _layer_spec(layer, (1, D_MODEL)),
            _layer_spec(layer, (D_MODEL, D_MODEL)),
            pl.BlockSpec((None, m, D_MODEL), lambda i, j: (i, 0, 0)),
            pl.BlockSpec((None, m, D_MODEL), lambda i, j: (i, 0, 0)),
            _layer_spec(layer, (D_MODEL, D_MODEL)),
            pl.BlockSpec((1, D_MODEL), lambda i, j: (0, 0)),
        ],
        out_specs=pl.BlockSpec((None, tm, D_MODEL), lambda i, j: (i, j, 0)),
        compiler_params=pltpu.CompilerParams(
            dimension_semantics=("arbitrary", "arbitrary"), vmem_limit_bytes=OUTATTN_VMEM_LIMIT),
        name="outattn",
    )(mix, proj, proj, proj, x, hnorm, gnorm, mnorm, skip, wout, gx, wq, kmem, vmem, wo, gf)


def _pack_w_in(w):
    ga0 = 2048 + 2 * N_HEADS * GLA_DK + N_HEADS * GLA_DV
    ga1 = ga0 + GLA_RANK
    gate0 = ga1 + N_HEADS * GLA_DV + 2 * N_HEADS * MLSTM_DH
    assert ga0 == C_GZ
    pad = jnp.zeros(w.shape[:-1] + (LANES - GLA_RANK - 2 * N_HEADS,), w.dtype)
    tail = jnp.concatenate([w[..., ga1:gate0], w[..., ga0:ga1], w[..., gate0:], pad], axis=-1)
    return w[..., :ga0].astype(BF16), tail.astype(BF16)


def _block_diag_dense(w):
    lead = w.shape[:-3]
    blk = w.shape[-1]
    rows = w.reshape(lead + (N_HEADS, MLSTM_DH, blk))
    tiled = jnp.tile(rows, (1,) * (len(lead) + 2) + (MLSTM_DH // blk,))
    r = np.arange(MLSTM_DH)
    on_diag = jnp.asarray((r[:, None] // blk) == (r[None, :] // blk))
    return jnp.where(on_diag, tiled, 0.0).astype(BF16)


def kernel(x, mem, norm_mix, w_in, hgrn_lb_logits, hgrn_norm, gla_gate_up, gla_gate_bias, gla_norm,
           mlstm_conv_w, mlstm_conv_b, mlstm_wq, mlstm_wk, mlstm_wv, mlstm_igate_bias,
           mlstm_fgate_bias, mlstm_skip, mlstm_norm, w_out, norm_xattn, norm_mem, xa_wq, xa_wk,
           xa_wv, xa_wo, norm_final):
    b, s, d = x.shape
    m = mem.shape[1]
    depth = w_in.shape[0]
    tri = jnp.asarray(np.tril(np.ones((CHUNK, CHUNK), np.float32)), dtype=BF16)
    mem2d = mem.reshape(b * m, d)

    def row(p):
        return p[:, None, :]

    w_in_a, w_in_b = _pack_w_in(w_in)
    gup = jnp.concatenate(
        [gla_gate_up, jnp.zeros((depth, LANES - GLA_RANK, N_HEADS * GLA_DK), F32)], axis=1).astype(BF16)
    sbias = row(jnp.concatenate(
        [jnp.zeros((depth, GLA_RANK), F32), mlstm_igate_bias, mlstm_fgate_bias,
         jnp.zeros((depth, LANES - GLA_RANK - 2 * N_HEADS), F32)], axis=1))
    wqkv = _block_diag_dense(jnp.stack([mlstm_wq, mlstm_wk, mlstm_wv]))
    w_out_b, xa_wq_b, xa_wk_b, xa_wv_b, xa_wo_b = (
        t.astype(BF16) for t in (w_out, xa_wq, xa_wk, xa_wv, xa_wo))

    for l in range(depth):
        proj = _inproj(x.reshape(b * s, d), row(norm_mix), w_in_a, w_in_b, hgrn_lb_logits, gup,
                       row(gla_gate_bias), sbias, layer=l,
                       tm=min(512, b * s)).reshape(b, s, PROJ_COLS)
        mix = _mixers(proj, tri, mlstm_conv_w, row(mlstm_conv_b), wqkv, layer=l)
        kmem, vmem = _memkv(mem2d, row(norm_mem), xa_wk_b, xa_wv_b, layer=l, tm=256)
        x = _outattn(mix, proj, x, row(hgrn_norm), row(gla_norm), row(mlstm_norm), row(mlstm_skip),
                     w_out_b, row(norm_xattn), xa_wq_b, kmem.reshape(b, m, d), vmem.reshape(b, m, d),
                     xa_wo_b, norm_final[None, :], layer=l, tm=min(1024, s),
                     final_norm=(l == depth - 1))
    return x
```
